```python
import math
import jax, jax.numpy as jnp
from jax import lax
import numpy as np

D_MODEL = 4096
BATCH = 2
SEQ = 8192
DEPTH = 4

HEAD_DIM = 128
GDN_HEADS = 16
FOX_HEADS = 16
GDN_WIDTH = GDN_HEADS * HEAD_DIM
FOX_WIDTH = FOX_HEADS * HEAD_DIM
MIX_WIDTH = GDN_WIDTH + FOX_WIDTH
CONV_WIDTH = 4
GDN_CHUNK = 64
FOX_BLOCK = 128
S5_GROUP = 16
S5_GROUPS = D_MODEL // S5_GROUP
S5_STATE = 64
S5_CHUNK = 128
D_FF = 4 * D_MODEL
N_EVEN = (DEPTH + 1) // 2
N_ODD = DEPTH // 2
RMS_EPS = 1e-6
L2_EPS = 1e-6
IN_SPLITS = [int(s) for s in np.cumsum([3 * GDN_WIDTH, GDN_WIDTH, GDN_HEADS, GDN_HEADS, FOX_WIDTH, FOX_WIDTH, FOX_WIDTH])]
IN_COLS = 4 * GDN_WIDTH + 2 * GDN_HEADS + 3 * FOX_WIDTH + FOX_HEADS

kernel_name = "hybrid_gdn_fox_s5_trunk"


def rms_norm(x, w):
    xf = x.astype(jnp.float32)
    y = xf * lax.rsqrt(jnp.mean(xf * xf, axis=-1, keepdims=True) + RMS_EPS)
    return (y * w.astype(jnp.float32)).astype(x.dtype)


def l2_normalize(x):
    return x * lax.rsqrt(jnp.sum(x * x, axis=-1, keepdims=True) + L2_EPS)


def causal_depthwise_conv(x, w):
    return lax.conv_general_dilated(
        x, w[:, None, :].astype(x.dtype), window_strides=(1,),
        padding=((CONV_WIDTH - 1, 0),), dimension_numbers=("NWC", "WIO", "NWC"),
        feature_group_count=x.shape[-1])


def gated_delta_rule(q, k, v, g, beta):
    B_, S_, H, Dk = q.shape
    Dv = v.shape[-1]
    n = S_ // GDN_CHUNK

    def chunks(t):
        return t.reshape(B_, n, GDN_CHUNK, H, t.shape[-1]).transpose(1, 0, 3, 2, 4)

    qc, kc, vc = chunks(q), chunks(k), chunks(v)
    gc = g.reshape(B_, n, GDN_CHUNK, H).transpose(1, 0, 3, 2)
    bc = beta.reshape(B_, n, GDN_CHUNK, H).transpose(1, 0, 3, 2)
    gcum = jnp.cumsum(gc, axis=-1)
    idx = jnp.arange(GDN_CHUNK)
    causal = idx[:, None] >= idx[None, :]
    strict = idx[:, None] > idx[None, :]
    decay_mat = jnp.exp(jnp.where(causal, gcum[..., :, None] - gcum[..., None, :], -jnp.inf))
    kb = kc * bc[..., None]
    m = jnp.where(strict, jnp.einsum("nbhid,nbhjd->nbhij", kb, kc) * decay_mat, 0.0)
    tri = m + jnp.eye(GDN_CHUNK, dtype=jnp.float32)
    rhs = jnp.concatenate([vc * bc[..., None], kb * jnp.exp(gcum)[..., None]], axis=-1)
    sol = lax.linalg.triangular_solve(tri, rhs, left_side=True, lower=True, unit_diagonal=True)
    u, w = sol[..., :Dv], sol[..., Dv:]
    attn_intra = jnp.einsum("nbhid,nbhjd->nbhij", qc, kc) * decay_mat

    def step(state, inp):
        q_i, k_i, u_i, w_i, a_i, g_i = inp
        v_new = u_i - jnp.einsum("bhcd,bhde->bhce", w_i, state)
        o = (jnp.einsum("bhcd,bhde->bhce", q_i * jnp.exp(g_i)[..., None], state)
             + jnp.einsum("bhij,bhje->bhie", a_i, v_new))
        g_last = g_i[..., -1]
        state = (state * jnp.exp(g_last)[..., None, None]
                 + jnp.einsum("bhcd,bhce->bhde", k_i * jnp.exp(g_last[..., None] - g_i)[..., None], v_new))
        return state, o

    s0 = jnp.zeros((B_, H, Dk, Dv), jnp.float32)
    _, o = lax.scan(step, s0, (qc, kc, u, w, attn_intra, gcum))
    return o.transpose(1, 0, 3, 2, 4).reshape(B_, S_, H, Dv)


def forgetting_attention(q, k, v, log_f):
    B_, S_, H, Dh = q.shape
    n = S_ // FOX_BLOCK
    c = jnp.cumsum(log_f, axis=1)
    qb = q.reshape(B_, n, FOX_BLOCK, H, Dh).transpose(1, 0, 3, 2, 4)
    cq = c.reshape(B_, n, FOX_BLOCK, H).transpose(1, 0, 3, 2)
    kh = k.transpose(0, 2, 1, 3)
    vh = v.transpose(0, 2, 1, 3)
    ck = c.transpose(0, 2, 1)
    k_pos = jnp.arange(S_)
    scale = Dh ** -0.5

    def attend_block(args):
        q_i, cq_i, blk = args
        s = jnp.einsum("bhqd,bhkd->bhqk", q_i, kh, preferred_element_type=jnp.float32) * scale
        s = s + (cq_i[..., :, None] - ck[:, :, None, :])
        q_pos = blk * FOX_BLOCK + jnp.arange(FOX_BLOCK)
        s = jnp.where(k_pos[None, :] <= q_pos[:, None], s, -jnp.inf)
        p = jax.nn.softmax(s, axis=-1)
        return jnp.einsum("bhqk,bhkd->bhqd", p.astype(v.dtype), vh)

    o = lax.map(attend_block, (qb, cq, jnp.arange(n)))
    return o.transpose(1, 0, 3, 2, 4).reshape(B_, S_, H * Dh)


def even_mixer(h, w_in, conv_w, a_log, dt_bias, gdn_norm_w, f_bias, w_out):
    B_, S_, _ = h.shape
    f32 = jnp.float32
    proj = h @ w_in
    gdn_qkv, gdn_z, gdn_b, gdn_a, fq, fk, fv, ff = jnp.split(proj, IN_SPLITS, axis=-1)

    qkv = jax.nn.silu(causal_depthwise_conv(gdn_qkv, conv_w)).astype(f32)
    q, k, v = [t.reshape(B_, S_, GDN_HEADS, HEAD_DIM) for t in jnp.split(qkv, 3, axis=-1)]
    q = l2_normalize(q) * (HEAD_DIM ** -0.5)
    k = l2_normalize(k)
    beta = jax.nn.sigmoid(gdn_b.astype(f32))
    g = -jnp.exp(a_log.astype(f32)) * jax.nn.softplus(gdn_a.astype(f32) + dt_bias.astype(f32))
    o = gated_delta_rule(q, k, v, g, beta)
    z = gdn_z.astype(f32).reshape(B_, S_, GDN_HEADS, HEAD_DIM)
    o = o * lax.rsqrt(jnp.mean(o * o, axis=-1, keepdims=True) + RMS_EPS) * gdn_norm_w.astype(f32) * jax.nn.silu(z)
    o_gdn = o.reshape(B_, S_, GDN_WIDTH).astype(h.dtype)

    log_f = jax.nn.log_sigmoid(ff.astype(f32) + f_bias.astype(f32))
    shp = (B_, S_, FOX_HEADS, HEAD_DIM)
    o_fox = forgetting_attention(fq.reshape(shp), fk.reshape(shp), fv.reshape(shp), log_f).astype(h.dtype)

    return jnp.concatenate([o_gdn, o_fox], axis=-1) @ w_out


def s5_mixer(u, lam_re, lam_im, b_re, b_im, c_re, c_im, d_skip, log_dt, w_glu):
    B_, S_, D_ = u.shape
    f32 = jnp.float32
    lam = lax.complex(jnp.minimum(lam_re.astype(f32), -1e-4), lam_im.astype(f32))
    dt = jnp.exp(log_dt.astype(f32))[:, None]
    lam_dt = lam * dt
    lam_bar = jnp.exp(lam_dt)
    b_bar = ((lam_bar - 1.0) / lam)[:, :, None] * lax.complex(b_re.astype(f32), b_im.astype(f32))
    c = lax.complex(c_re.astype(f32), c_im.astype(f32))
    steps = jnp.arange(1, S5_CHUNK + 1, dtype=f32)
    lam_pows = jnp.exp(lam_dt[None] * steps[:, None, None])
    d = d_skip.astype(f32).reshape(S5_GROUPS, S5_GROUP)
    n = S_ // S5_CHUNK
    uc = u.astype(f32).reshape(B_, n, S5_CHUNK, S5_GROUPS, S5_GROUP).transpose(1, 0, 2, 3, 4)

    def combine(e1, e2):
        a1, x1 = e1
        a2, x2 = e2
        return a1 * a2, a2 * x1 + x2

    def segment(h0, u_i):
        bu = jnp.einsum("blgc,gpc->blgp", u_i.astype(jnp.complex64), b_bar)
        a = jnp.broadcast_to(lam_bar, bu.shape)
        _, xs = lax.associative_scan(combine, (a, bu), axis=1)
        xs = xs + lam_pows[None] * h0[:, None]
        y = jnp.einsum("gcp,blgp->blgc", c, xs).real + d * u_i
        return xs[:, -1], y

    h0 = jnp.zeros((B_, S5_GROUPS, S5_STATE), jnp.complex64)
    _, y = lax.scan(segment, h0, uc)
    y = y.transpose(1, 0, 2, 3, 4).reshape(B_, S_, D_)
    y = jax.nn.gelu(y).astype(u.dtype)
    val, gate = jnp.split(y @ w_glu, 2, axis=-1)
    return val * jax.nn.sigmoid(gate)


def squared_relu_mlp(h, w_up, w_down):
    return jnp.square(jax.nn.relu(h @ w_up)) @ w_down


def setup_inputs(seed: int = 0) -> dict:
    key = jax.random.key(seed)
    ks = jax.random.split(key, 24)
    f32 = jnp.float32

    def nrm(k, shape, scale):
        return jax.random.normal(k, shape, f32) * scale

    x = nrm(ks[0], (BATCH, SEQ, D_MODEL), 1.0)
    norm_mix = 1.0 + nrm(ks[1], (DEPTH, D_MODEL), 0.02)
    norm_mlp = 1.0 + nrm(ks[2], (DEPTH, D_MODEL), 0.02)
    norm_final = 1.0 + nrm(ks[3], (D_MODEL,), 0.02)
    w_in = nrm(ks[4], (N_EVEN, D_MODEL, IN_COLS), D_MODEL ** -0.5)
    conv_qkv = nrm(ks[5], (N_EVEN, CONV_WIDTH, 3 * GDN_WIDTH), CONV_WIDTH ** -0.5)
    gdn_a_log = jnp.log(jax.random.uniform(ks[6], (N_EVEN, GDN_HEADS), f32, 1.0, 16.0))
    dt0 = jnp.exp(jax.random.uniform(ks[7], (N_EVEN, GDN_HEADS), f32, math.log(1e-3), math.log(1e-1)))
    gdn_dt_bias = dt0 + jnp.log(-jnp.expm1(-dt0))
    gdn_norm = 1.0 + nrm(ks[8], (N_EVEN, HEAD_DIM), 0.02)
    fox_f_bias = jax.random.uniform(ks[9], (N_EVEN, FOX_HEADS), f32, 1.0, 4.0)
    w_out = nrm(ks[10], (N_EVEN, MIX_WIDTH, D_MODEL), MIX_WIDTH ** -0.5)
    s5_lambda_re = -0.5 + nrm(ks[11], (N_ODD, S5_GROUPS, S5_STATE), 0.01)
    s5_lambda_im = math.pi * jnp.arange(S5_STATE, dtype=f32) + nrm(ks[12], (N_ODD, S5_GROUPS, S5_STATE), 0.01)
    s5_b_re = nrm(ks[13], (N_ODD, S5_GROUPS, S5_STATE, S5_GROUP), (2 * S5_GROUP) ** -0.5)
    s5_b_im = nrm(ks[14], (N_ODD, S5_GROUPS, S5_STATE, S5_GROUP), (2 * S5_GROUP) ** -0.5)
    s5_c_re = nrm(ks[15], (N_ODD, S5_GROUPS, S5_GROUP, S5_STATE), (2 * S5_STATE) ** -0.5)
    s5_c_im = nrm(ks[16], (N_ODD, S5_GROUPS, S5_GROUP, S5_STATE), (2 * S5_STATE) ** -0.5)
    s5_d = nrm(ks[17], (N_ODD, D_MODEL), 1.0)
    s5_log_dt = jax.random.uniform(ks[18], (N_ODD, S5_GROUPS), f32, math.log(1e-3), math.log(1e-1))
    w_glu = nrm(ks[19], (N_ODD, D_MODEL, 2 * D_MODEL), D_MODEL ** -0.5)
    w_up = nrm(ks[20], (DEPTH, D_MODEL, D_FF), D_MODEL ** -0.5)
    w_down = nrm(ks[21], (DEPTH, D_FF, D_MODEL), D_FF ** -0.5)
    return {
        "x": x, "norm_mix": norm_mix, "norm_mlp": norm_mlp, "norm_final": norm_final,
        "w_in": w_in, "conv_qkv": conv_qkv, "gdn_a_log": gdn_a_log, "gdn_dt_bias": gdn_dt_bias,
        "gdn_norm": gdn_norm, "fox_f_bias": fox_f_bias, "w_out": w_out,
        "s5_lambda_re": s5_lambda_re, "s5_lambda_im": s5_lambda_im, "s5_b_re": s5_b_re, "s5_b_im": s5_b_im,
        "s5_c_re": s5_c_re, "s5_c_im": s5_c_im, "s5_d": s5_d, "s5_log_dt": s5_log_dt, "w_glu": w_glu,
        "w_up": w_up, "w_down": w_down,
    }


def reference(x, norm_mix, norm_mlp, norm_final, w_in, conv_qkv, gdn_a_log, gdn_dt_bias, gdn_norm,
              fox_f_bias, w_out, s5_lambda_re, s5_lambda_im, s5_b_re, s5_b_im, s5_c_re, s5_c_im, s5_d,
              s5_log_dt, w_glu, w_up, w_down):
    h = x
    for layer in range(DEPTH):
        i = layer // 2
        hn = rms_norm(h, norm_mix[layer])
        if layer % 2 == 0:
            mixed = even_mixer(hn, w_in[i], conv_qkv[i], gdn_a_log[i], gdn_dt_bias[i], gdn_norm[i],
                               fox_f_bias[i], w_out[i])
        else:
            mixed = s5_mixer(hn, s5_lambda_re[i], s5_lambda_im[i], s5_b_re[i], s5_b_im[i], s5_c_re[i],
                             s5_c_im[i], s5_d[i], s5_log_dt[i], w_glu[i])
        h = h + mixed.astype(h.dtype)
        h = h + squared_relu_mlp(rms_norm(h, norm_mlp[layer]), w_up[layer], w_down[layer]).astype(h.dtype)
    return rms_norm(h, norm_final)
```

```python
import functools
import math

import jax
import jax.numpy as jnp
from jax import lax
from jax.experimental import pallas as pl
from jax.experimental.pallas import tpu as pltpu

F32 = jnp.float32
BF16 = jnp.bfloat16

LANES = 128
HEAD_DIM = 128
GDN_HEADS = 16
FOX_HEADS = 16
CONV_WIDTH = 4
GDN_CHUNK = 64
S5_GROUP = 16
S5_STATE = 64
S5_L = 16
S5_GPT = LANES // S5_GROUP
RMS_EPS = 1e-6
L2_EPS = 1e-6
VMEM_LIMIT = 60 * 1024 * 1024

SM_BETA, SM_A, SM_F = 0, 16, 32


def _cparams(n_axes):
    return pltpu.CompilerParams(dimension_semantics=("arbitrary",) * n_axes,
                                vmem_limit_bytes=VMEM_LIMIT)


def _rms_rows(x, w):
    ms = jnp.mean(x * x, axis=-1, keepdims=True)
    return x * lax.rsqrt(ms + RMS_EPS) * w


def _norm_to(h_ref, nw_ref, dst_ref, rows):
    tm = h_ref.shape[0]

    def body(r, c):
        sl = pl.ds(pl.multiple_of(r * rows, rows), rows)
        dst_ref[sl, :] = _rms_rows(h_ref[sl, :], nw_ref[...]).astype(dst_ref.dtype)
        return c

    lax.fori_loop(0, tm // rows, body, 0)


def _norm_proj_kernel(h_ref, nw_ref, wb_ref, ws_ref, big_ref, small_ref, hn_ref, *, tn):
    @pl.when(pl.program_id(1) == 0)
    def _():
        _norm_to(h_ref, nw_ref, hn_ref, 32)
        small_ref[...] = jnp.dot(hn_ref[...], ws_ref[...], preferred_element_type=F32)

    acc = jnp.dot(hn_ref[...], wb_ref[...], preferred_element_type=F32)
    for c in range(tn // LANES):
        big_ref[c] = acc[:, c * LANES:(c + 1) * LANES].astype(BF16)


def norm_proj(h, nw, wbig, wsmall, *, tm, tn):
    M, D = h.shape
    N = wbig.shape[1]
    return pl.pallas_call(
        functools.partial(_norm_proj_kernel, tn=tn),
        grid=(M // tm, N // tn),
        in_specs=[pl.BlockSpec((tm, D), lambda i, j: (i, 0)),
                  pl.BlockSpec((1, D), lambda i, j: (0, 0)),
                  pl.BlockSpec((D, tn), lambda i, j: (0, j)),
                  pl.BlockSpec((D, LANES), lambda i, j: (0, 0))],
        out_specs=[pl.BlockSpec((tn // LANES, tm, LANES), lambda i, j: (j, i, 0)),
                   pl.BlockSpec((tm, LANES), lambda i, j: (i, 0))],
        out_shape=[jax.ShapeDtypeStruct((N // LANES, M, LANES), BF16),
                   jax.ShapeDtypeStruct((M, LANES), F32)],
        scratch_shapes=[pltpu.VMEM((tm, D), BF16)],
        compiler_params=_cparams(2),
        name="norm_proj",
    )(h, nw, wbig, wsmall)


def _softplus(x):
    return jnp.maximum(x, 0.0) + jnp.log1p(jnp.exp(-jnp.abs(x)))


def _prep_kernel(qkv_ref, prev_ref, sm_ref, cw_ref, prm_ref,
                 q_ref, qg_ref, k_ref, kw_ref, kd_ref, vb_ref,
                 db_ref, dc_ref, el_ref, cq_ref, ck_ref, carry_ref, *, tp):
    t = pl.program_id(1)
    nh = GDN_HEADS
    nck = tp // GDN_CHUNK

    @pl.when(t == 0)
    def _():
        carry_ref[...] = jnp.zeros_like(carry_ref)

    sm = sm_ref[...]
    a_log = prm_ref[0:1, :]
    dt_bias = prm_ref[1:2, :]
    f_bias = prm_ref[2:3, :]
    beta = jax.nn.sigmoid(sm)
    g = -jnp.exp(a_log) * _softplus(sm + dt_bias)
    logf = -_softplus(-(sm + f_bias))

    row = lax.broadcasted_iota(jnp.int32, (tp, tp), 0)
    col = lax.broadcasted_iota(jnp.int32, (tp, tp), 1)
    tri_full = (row >= col).astype(F32)
    csh = GDN_CHUNK.bit_length() - 1
    tri_chunk = jnp.where((row >> csh) == (col >> csh), tri_full, 0.0)
    hi = lax.Precision.HIGHEST
    gcum = jnp.dot(tri_chunk, g, precision=hi, preferred_element_type=F32)
    ccum = jnp.dot(tri_full, logf, precision=hi, preferred_element_type=F32) + carry_ref[0:1, :]
    carry_ref[0:1, :] = ccum[tp - 1:tp, :]

    glast = jnp.concatenate(
        [jnp.broadcast_to(gcum[(c + 1) * GDN_CHUNK - 1:(c + 1) * GDN_CHUNK, :], (GDN_CHUNK, LANES))
         for c in range(nck)], axis=0)
    exp_g = jnp.exp(gcum)
    exp_kd = jnp.exp(glast - gcum)
    exp_last = jnp.exp(glast)

    lane = lax.broadcasted_iota(jnp.int32, (tp, LANES), 1)
    comb = jnp.where((lane >= SM_A) & (lane < SM_A + nh), gcum, ccum)
    comb_t = comb.T

    ntile = qkv_ref.shape[0]
    prev_ok = (t > 0).astype(F32)
    ci = lax.broadcasted_iota(jnp.int32, (GDN_CHUNK, GDN_CHUNK), 0)
    cj = lax.broadcasted_iota(jnp.int32, (GDN_CHUNK, GDN_CHUNK), 1)

    def conv_tile(idx):
        cur = qkv_ref[idx].astype(F32)
        tail = prev_ref[idx].astype(F32)[8:16, :] * prev_ok
        xc = jnp.concatenate([tail, cur], axis=0)
        w = cw_ref[:, idx, :]
        acc = xc[8:8 + tp, :] * w[3:4, :]
        for j in range(CONV_WIDTH - 1):
            sh = CONV_WIDTH - 1 - j
            acc = acc + xc[8 - sh:8 - sh + tp, :] * w[j:j + 1, :]
        return acc * jax.nn.sigmoid(acc)

    def l2n(x):
        return x * lax.rsqrt(jnp.sum(x * x, axis=-1, keepdims=True) + L2_EPS)

    scale = HEAD_DIM ** -0.5
    for h in range(nh):
        qh = l2n(conv_tile(h)) * scale
        kh = l2n(conv_tile(nh + h))
        vh = conv_tile(2 * nh + h)
        b_col = beta[:, SM_BETA + h:SM_BETA + h + 1]
        eg_col = exp_g[:, SM_A + h:SM_A + h + 1]
        ekd_col = exp_kd[:, SM_A + h:SM_A + h + 1]
        q_ref[h] = qh.astype(BF16)
        qg_ref[h] = (qh * eg_col).astype(BF16)
        k_ref[h] = kh.astype(BF16)
        kw_ref[h] = (kh * (b_col * eg_col)).astype(BF16)
        kd_ref[h] = (kh * ekd_col).astype(BF16)
        vb_ref[h] = (vh * b_col).astype(BF16)
        g_col = gcum[:, SM_A + h:SM_A + h + 1]
        for c in range(nck):
            rs = slice(c * GDN_CHUNK, (c + 1) * GDN_CHUNK)
            diff = g_col[rs, :] - comb_t[SM_A + h:SM_A + h + 1, rs]
            dec = jnp.exp(jnp.where(ci >= cj, diff, -jnp.inf))
            dc_ref[0, h, c] = dec
            db_ref[0, h, c] = jnp.where(ci > cj, dec * b_col[rs, :], 0.0)
            el_ref[0, h, c] = exp_last[c * GDN_CHUNK:c * GDN_CHUNK + 1, SM_A + h:SM_A + h + 1] * jnp.ones((1, LANES), F32)
        cq_ref[0, h] = jnp.broadcast_to(ccum[:, SM_F + h:SM_F + h + 1], (tp, LANES))
        ck_ref[0, h] = comb_t[SM_F + h:SM_F + h + 1, :]


def gdn_prep(big, small, conv_w, prm, *, B, S, tp):
    M = B * S
    nt = S // tp
    nh = GDN_HEADS
    nqkv = 3 * nh
    nck = tp // GDN_CHUNK
    NC = S // GDN_CHUNK
    kern = functools.partial(_prep_kernel, tp=tp)
    head_spec = pl.BlockSpec((nh, tp, LANES), lambda b, t: (0, b * nt + t, 0))
    head_shape = jax.ShapeDtypeStruct((nh, M, LANES), BF16)
    return pl.pallas_call(
        kern,
        grid=(B, nt),
        in_specs=[pl.BlockSpec((nqkv, tp, LANES), lambda b, t: (0, b * nt + t, 0)),
                  pl.BlockSpec((nqkv, 16, LANES),
                               lambda b, t: (0, jnp.maximum((b * nt + t) * (tp // 16) - 1, 0), 0)),
                  pl.BlockSpec((tp, LANES), lambda b, t: (b * nt + t, 0)),
                  pl.BlockSpec((CONV_WIDTH, nqkv, LANES), lambda b, t: (0, 0, 0)),
                  pl.BlockSpec((8, LANES), lambda b, t: (0, 0))],
        out_specs=[head_spec] * 6 + [
            pl.BlockSpec((1, nh, nck, GDN_CHUNK, GDN_CHUNK), lambda b, t: (b, 0, t, 0, 0)),
            pl.BlockSpec((1, nh, nck, GDN_CHUNK, GDN_CHUNK), lambda b, t: (b, 0, t, 0, 0)),
            pl.BlockSpec((1, nh, nck, 1, LANES), lambda b, t: (b, 0, t, 0, 0)),
            pl.BlockSpec((1, nh, tp, LANES), lambda b, t: (b, 0, t, 0)),
            pl.BlockSpec((1, nh, 1, tp), lambda b, t: (b, 0, 0, t)),
        ],
        out_shape=[head_shape] * 6 + [
            jax.ShapeDtypeStruct((B, nh, NC, GDN_CHUNK, GDN_CHUNK), F32),
            jax.ShapeDtypeStruct((B, nh, NC, GDN_CHUNK, GDN_CHUNK), F32),
            jax.ShapeDtypeStruct((B, nh, NC, 1, LANES), F32),
            jax.ShapeDtypeStruct((B, nh, S, LANES), F32),
            jax.ShapeDtypeStruct((B, nh, 1, S), F32),
        ],
        scratch_shapes=[pltpu.VMEM((8, LANES), F32)],
        compiler_params=_cparams(2),
        name="gdn_prep",
    )(big, big, small, conv_w, prm)


def _bdot(a, b):
    return lax.dot_general(a, b, (((2,), (1,)), ((0,), (0,))), preferred_element_type=F32)


def _gdn_kernel(q_ref, qg_ref, k_ref, kw_ref, kd_ref, vb_ref, z_ref, db_ref, dc_ref, el_ref, nw_ref,
                o_ref, s_ref, *, tg):
    C = GDN_CHUNK
    nck = tg // C

    @pl.when(pl.program_id(2) == 0)
    def _():
        s_ref[...] = jnp.zeros_like(s_ref)

    k3 = k_ref[0].reshape(nck, C, HEAD_DIM)
    q3 = q_ref[0].reshape(nck, C, HEAD_DIM)
    kq = jnp.concatenate([k3, q3], axis=1)
    p = lax.dot_general(kq, k3, (((2,), (2,)), ((0,), (0,))), preferred_element_type=F32)
    m = p[:, :C, :] * db_ref[0, 0]
    attn = (p[:, C:, :] * dc_ref[0, 0]).astype(BF16)

    ii = lax.broadcasted_iota(jnp.int32, (C, C), 0)
    jj = lax.broadcasted_iota(jnp.int32, (C, C), 1)
    x = jnp.broadcast_to((ii == jj).astype(F32), (nck, C, C))
    s = 1
    while s < C:
        ls = s.bit_length() - 1
        sel = ((ii >> (ls + 1)) == (jj >> (ls + 1))) & (((ii >> ls) & 1) == 1) & (((jj >> ls) & 1) == 0)
        cs = jnp.where(sel, m, 0.0).astype(BF16)
        xb = x.astype(BF16)
        x = x - _bdot(_bdot(xb, cs).astype(BF16), xb)
        s *= 2
    xb = x.astype(BF16)
    u = _bdot(xb, vb_ref[0].reshape(nck, C, HEAD_DIM))
    w = _bdot(xb, kw_ref[0].reshape(nck, C, HEAD_DIM)).astype(BF16)

    nw = nw_ref[...]
    st = s_ref[...]
    for c in range(nck):
        rs = slice(c * C, (c + 1) * C)
        sb = st.astype(BF16)
        v_new = u[c] - jnp.dot(w[c], sb, preferred_element_type=F32)
        vb16 = v_new.astype(BF16)
        o = (jnp.dot(qg_ref[0, rs, :], sb, preferred_element_type=F32)
             + jnp.dot(attn[c], vb16, preferred_element_type=F32))
        st = st * el_ref[0, 0, c] + lax.dot_general(kd_ref[0, rs, :], vb16, (((0,), (0,)), ((), ())),
                                                    preferred_element_type=F32)
        z = z_ref[0, rs, :].astype(F32)
        o = _rms_rows(o, nw) * (z * jax.nn.sigmoid(z))
        o_ref[rs, :] = o.astype(BF16)
    s_ref[...] = st


def gdn_core(q, qg, k, kw, kd, vb, big, db, dc, el, gdn_nw, *, B, S, tg):
    M = B * S
    nt = S // tg
    nh = GDN_HEADS
    nck = tg // GDN_CHUNK
    z_tile0 = 3 * nh
    hs = pl.BlockSpec((1, tg, LANES), lambda b, h, t: (h, b * nt + t, 0))
    ds_ = pl.BlockSpec((1, 1, nck, GDN_CHUNK, GDN_CHUNK), lambda b, h, t: (b, h, t, 0, 0))
    return pl.pallas_call(
        functools.partial(_gdn_kernel, tg=tg),
        grid=(B, nh, nt),
        in_specs=[hs] * 6 + [
            pl.BlockSpec((1, tg, LANES), lambda b, h, t: (z_tile0 + h, b * nt + t, 0)),
            ds_, ds_,
            pl.BlockSpec((1, 1, nck, 1, LANES), lambda b, h, t: (b, h, t, 0, 0)),
            pl.BlockSpec((1, LANES), lambda b, h, t: (0, 0))],
        out_specs=pl.BlockSpec((tg, LANES), lambda b, h, t: (b * nt + t, h)),
        out_shape=jax.ShapeDtypeStruct((M, nh * HEAD_DIM), BF16),
        scratch_shapes=[pltpu.VMEM((HEAD_DIM, HEAD_DIM), F32)],
        compiler_params=_cparams(3),
        name="gdn_core",
    )(q, qg, k, kw, kd, vb, big, db, dc, el, gdn_nw)


def _fox_kernel(q_ref, k_ref, v_ref, cq_ref, ck_ref, o_ref, *, tq):
    i = pl.program_id(2)
    scale = HEAD_DIM ** -0.5
    q = q_ref[0]
    cq = cq_ref[0, 0]
    if tq > LANES:
        cq = jnp.concatenate([cq] * (tq // LANES), axis=1)

    def block(j, carry, masked):
        m_prev, l_prev, acc = carry
        ks = pl.ds(pl.multiple_of(j * tq, tq), tq)
        kj = k_ref[0, ks, :]
        vj = v_ref[0, ks, :]
        s = lax.dot_general(q, kj, (((1,), (1,)), ((), ())), preferred_element_type=F32) * scale
        s = s + (cq - ck_ref[0, 0, j])
        if masked:
            r = lax.broadcasted_iota(jnp.int32, (tq, tq), 0)
            c = lax.broadcasted_iota(jnp.int32, (tq, tq), 1)
            s = jnp.where(c <= r, s, -jnp.inf)
        m_new = jnp.maximum(m_prev, jnp.max(s, axis=-1, keepdims=True))
        alpha = jnp.exp(m_prev - m_new)
        pexp = jnp.exp(s - m_new)
        l_new = alpha * l_prev + jnp.sum(pexp, axis=-1, keepdims=True)
        acc = alpha * acc + jnp.dot(pexp.astype(BF16), vj, preferred_element_type=F32)
        return m_new, l_new, acc

    init = (jnp.full((tq, 1), -jnp.inf, F32), jnp.zeros((tq, 1), F32), jnp.zeros((tq, HEAD_DIM), F32))
    carry = lax.fori_loop(0, i, lambda j, c: block(j, c, False), init)
    _, l_fin, acc = block(i, carry, True)
    o_ref[...] = (acc / l_fin).astype(BF16)


def fox_attention(big, cq, ck, *, B, S, tq):
    M = B * S
    nq = S // tq
    ck = ck.reshape(B, FOX_HEADS, nq, 1, tq)
    nh = FOX_HEADS
    q0 = 4 * GDN_HEADS
    k0 = q0 + nh
    v0 = k0 + nh
    return pl.pallas_call(
        functools.partial(_fox_kernel, tq=tq),
        grid=(B, nh, nq),
        in_specs=[pl.BlockSpec((1, tq, LANES), lambda b, h, i: (q0 + h, b * nq + i, 0)),
                  pl.BlockSpec((1, S, LANES), lambda b, h, i: (k0 + h, b, 0)),
                  pl.BlockSpec((1, S, LANES), lambda b, h, i: (v0 + h, b, 0)),
                  pl.BlockSpec((1, 1, tq, LANES), lambda b, h, i: (b, h, i, 0)),
                  pl.BlockSpec((1, 1, nq, 1, tq), lambda b, h, i: (b, h, 0, 0, 0))],
        out_specs=pl.BlockSpec((tq, LANES), lambda b, h, i: (b * nq + i, h)),
        out_shape=jax.ShapeDtypeStruct((M, nh * HEAD_DIM), BF16),
        compiler_params=_cparams(3),
        name="fox_attention",
    )(big, big, big, cq, ck)


def _out_proj_kernel(og_ref, of_ref, w1_ref, w2_ref, h_ref, o_ref):
    acc = jnp.dot(og_ref[...], w1_ref[...], preferred_element_type=F32)
    acc = acc + jnp.dot(of_ref[...], w2_ref[...], preferred_element_type=F32)
    o_ref[...] = h_ref[...] + acc


def out_proj(og, of, w_out, h, *, tm, tn):
    M, D = h.shape
    Kg = og.shape[1]
    nkb = Kg // of.shape[1]
    assert nkb == 1
    return pl.pallas_call(
        _out_proj_kernel,
        grid=(M // tm, D // tn),
        in_specs=[pl.BlockSpec((tm, Kg), lambda i, j: (i, 0)),
                  pl.BlockSpec((tm, Kg), lambda i, j: (i, 0)),
                  pl.BlockSpec((Kg, tn), lambda i, j: (0, j)),
                  pl.BlockSpec((Kg, tn), lambda i, j: (1, j)),
                  pl.BlockSpec((tm, tn), lambda i, j: (i, j))],
        out_specs=pl.BlockSpec((tm, tn), lambda i, j: (i, j)),
        out_shape=jax.ShapeDtypeStruct((M, D), F32),
        compiler_params=_cparams(2),
        name="out_proj",
    )(og, of, w_out, w_out, h)


def _mlp_kernel(h_ref, nw_ref, wu_ref, wd_ref, fw_ref, o_ref, hn_ref, *, final_norm):
    f = pl.program_id(1)

    @pl.when(f == 0)
    def _():
        _norm_to(h_ref, nw_ref, hn_ref, 32)
        o_ref[...] = h_ref[...]

    a = jnp.dot(hn_ref[...], wu_ref[...], preferred_element_type=F32)
    a = jnp.maximum(a, 0.0)
    a = (a * a).astype(BF16)
    o_ref[...] += jnp.dot(a, wd_ref[...], preferred_element_type=F32)

    if final_norm:
        @pl.when(f == pl.num_programs(1) - 1)
        def _():
            _norm_to(o_ref, fw_ref, o_ref, 32)


def mlp(h, nw, w_up, w_down, fw, *, tm, tf, final_norm):
    M, D = h.shape
    F = w_up.shape[1]
    return pl.pallas_call(
        functools.partial(_mlp_kernel, final_norm=final_norm),
        grid=(M // tm, F // tf),
        in_specs=[pl.BlockSpec((tm, D), lambda i, f: (i, 0)),
                  pl.BlockSpec((1, D), lambda i, f: (0, 0)),
                  pl.BlockSpec((D, tf), lambda i, f: (0, f)),
                  pl.BlockSpec((tf, D), lambda i, f: (f, 0)),
                  pl.BlockSpec((1, D), lambda i, f: (0, 0))],
        out_specs=pl.BlockSpec((tm, D), lambda i, f: (i, 0)),
        out_shape=jax.ShapeDtypeStruct((M, D), F32),
        scratch_shapes=[pltpu.VMEM((tm, D), BF16)],
        compiler_params=_cparams(2),
        name="mlp",
    )(h, nw, w_up, w_down, fw)


def _rmsnorm_kernel(h_ref, nw_ref, o_ref):
    _norm_to(h_ref, nw_ref, o_ref, 32)


def rmsnorm(h, nw, *, tm):
    M, D = h.shape
    return pl.pallas_call(
        _rmsnorm_kernel,
        grid=(M // tm,),
        in_specs=[pl.BlockSpec((tm, D), lambda i: (i, 0)),
                  pl.BlockSpec((1, D), lambda i: (0, 0))],
        out_specs=pl.BlockSpec((tm, D), lambda i: (i, 0)),
        out_shape=jax.ShapeDtypeStruct((M, D), F32),
        compiler_params=_cparams(1),
        name="rmsnorm",
    )(h, nw)


def _s5_kernel(u_ref, wt_ref, wi_ref, wo_ref, ap_ref, d_ref, y_ref, x_ref, yn_ref, carry_ref, *, ts):
    L = S5_L
    ncb = ts // L
    nst = S5_GPT * S5_STATE

    @pl.when(pl.program_id(2) == 0)
    def _():
        carry_ref[...] = jnp.zeros_like(carry_ref)

    for s in range(L):
        x_ref[:, s * LANES:(s + 1) * LANES] = u_ref[pl.ds(s, ncb, stride=L), :].astype(BF16)
    xb = x_ref[...]
    y_intra = jnp.dot(xb, wt_ref[0], preferred_element_type=F32)
    v = jnp.dot(xb, wi_ref[0], preferred_element_type=F32)

    rows = lax.broadcasted_iota(jnp.int32, (ncb, nst), 0)
    vr, vi = v[:, :nst], v[:, nst:]
    er = jnp.where(rows == 0, carry_ref[0:1, :], pltpu.roll(vr, 1, axis=0))
    ei = jnp.where(rows == 0, carry_ref[1:2, :], pltpu.roll(vi, 1, axis=0))
    d = 1
    k = 1
    while d < ncb:
        ar = ap_ref[0, k:k + 1, :nst]
        ai = ap_ref[0, k:k + 1, nst:]
        sr = jnp.where(rows >= d, pltpu.roll(er, d, axis=0), 0.0)
        si = jnp.where(rows >= d, pltpu.roll(ei, d, axis=0), 0.0)
        er, ei = er + ar * sr - ai * si, ei + ar * si + ai * sr
        d *= 2
        k += 1
    a1r = ap_ref[0, 0:1, :nst]
    a1i = ap_ref[0, 0:1, nst:]
    lr, li = er[ncb - 1:ncb, :], ei[ncb - 1:ncb, :]
    carry_ref[0:1, :] = a1r * lr - a1i * li + vr[ncb - 1:ncb, :]
    carry_ref[1:2, :] = a1r * li + a1i * lr + vi[ncb - 1:ncb, :]

    hprev = jnp.concatenate([er, ei], axis=1).astype(BF16)
    y = y_intra + jnp.dot(hprev, wo_ref[0], preferred_element_type=F32)
    dsk = d_ref[0]
    for s in range(L):
        rs = pl.ds(s, ncb, stride=L)
        yn_ref[rs, :] = y[:, s * LANES:(s + 1) * LANES] + dsk * u_ref[rs, :]
    yn = yn_ref[...]
    c0 = math.sqrt(2.0 / math.pi)
    y_ref[...] = (0.5 * yn * (1.0 + jnp.tanh(c0 * (yn + 0.044715 * (yn * yn * yn))))).astype(BF16)


def s5_core(u, wt, wi, wo, apow, dskip, *, B, S, ts):
    M, D = u.shape
    nj = D // LANES
    ns = S // ts
    ncb = ts // S5_L
    nst = S5_GPT * S5_STATE
    return pl.pallas_call(
        functools.partial(_s5_kernel, ts=ts),
        grid=(nj, B, ns),
        in_specs=[pl.BlockSpec((ts, LANES), lambda j, b, s: (b * ns + s, j)),
                  pl.BlockSpec((1, S5_L * LANES, S5_L * LANES), lambda j, b, s: (j, 0, 0)),
                  pl.BlockSpec((1, S5_L * LANES, 2 * nst), lambda j, b, s: (j, 0, 0)),
                  pl.BlockSpec((1, 2 * nst, S5_L * LANES), lambda j, b, s: (j, 0, 0)),
                  pl.BlockSpec((1, apow.shape[1], 2 * nst), lambda j, b, s: (j, 0, 0)),
                  pl.BlockSpec((1, 1, LANES), lambda j, b, s: (j, 0, 0))],
        out_specs=pl.BlockSpec((ts, LANES), lambda j, b, s: (b * ns + s, j)),
        out_shape=jax.ShapeDtypeStruct((M, D), BF16),
        scratch_shapes=[pltpu.VMEM((ncb, S5_L * LANES), BF16),
                        pltpu.VMEM((ts, LANES), F32),
                        pltpu.VMEM((8, nst), F32)],
        compiler_params=_cparams(3),
        name="s5_core",
    )(u, wt, wi, wo, apow, dskip)


def s5_operators(lam_re, lam_im, b_re, b_im, c_re, c_im, d_skip, log_dt, ncb):
    G, P = lam_re.shape
    L = S5_L
    gc = S5_GROUP
    lam = lax.complex(jnp.minimum(lam_re.astype(F32), -1e-4), lam_im.astype(F32))
    dt = jnp.exp(log_dt.astype(F32))[:, None]
    lam_dt = lam * dt
    lam_bar = jnp.exp(lam_dt)
    b_bar = ((lam_bar - 1.0) / lam)[:, :, None] * lax.complex(b_re.astype(F32), b_im.astype(F32))
    c = lax.complex(c_re.astype(F32), c_im.astype(F32))
    kk = jnp.arange(L + 1, dtype=F32)
    pows = jnp.exp(lam_dt[None] * kk[:, None, None])

    kern = jnp.einsum("gcp,kgp,gpd->kgcd", c, pows[:L], b_bar).real
    sidx = jnp.arange(L)
    lag = sidx[None, :] - sidx[:, None]
    tmat = jnp.where((lag >= 0)[:, :, None, None, None], kern[jnp.clip(lag, 0, L - 1)], 0.0)
    tmat = tmat.transpose(2, 0, 4, 1, 3)

    min_c = pows[L - 1 - sidx][:, :, :, None] * b_bar[None]
    min_c = min_c.transpose(1, 0, 3, 2)
    mout_c = c[None] * pows[1:L + 1][:, :, None, :]
    mout_c = mout_c.transpose(1, 3, 0, 2)

    nj = G // S5_GPT
    eye = jnp.eye(S5_GPT, dtype=F32)
    t6 = tmat.reshape(nj, S5_GPT, L, gc, L, gc)
    wt = jnp.einsum("jgsdtc,gh->jsgdthc", t6, eye).reshape(nj, L * LANES, L * LANES).astype(BF16)
    mi = jnp.stack([min_c.real, min_c.imag], axis=3).reshape(nj, S5_GPT, L, gc, 2, P)
    wi = jnp.einsum("jgsdrp,gh->jsgdrhp", mi, eye).reshape(nj, L * LANES, 2 * S5_GPT * P).astype(BF16)
    mo = jnp.stack([mout_c.real, -mout_c.imag], axis=1).reshape(nj, S5_GPT, 2, P, L, gc)
    wo = jnp.einsum("jgrptc,gh->jrgpthc", mo, eye).reshape(nj, 2 * S5_GPT * P, L * LANES).astype(BF16)

    nsteps = max(1, int(math.log2(ncb)))
    mult = jnp.asarray([1.0] + [2.0 ** k for k in range(nsteps)], F32) * L
    ap = jnp.exp(lam_dt[None] * mult[:, None, None])
    ap = jnp.stack([ap.real, ap.imag], axis=1).reshape(1 + nsteps, 2, nj, S5_GPT * P)
    ap = ap.transpose(2, 0, 1, 3).reshape(nj, 1 + nsteps, 2 * S5_GPT * P)
    pad = (-ap.shape[1]) % 8
    ap = jnp.pad(ap, ((0, 0), (0, pad), (0, 0)))
    dsk = d_skip.astype(F32).reshape(nj, 1, LANES)
    return wt, wi, wo, ap, dsk


def _glu_kernel(y_ref, wv_ref, wg_ref, h_ref, o_ref):
    y = y_ref[...]
    val = jnp.dot(y, wv_ref[...], preferred_element_type=F32)
    gate = jnp.dot(y, wg_ref[...], preferred_element_type=F32)
    o_ref[...] = h_ref[...] + val * jax.nn.sigmoid(gate)


def glu_proj(y, w_glu, h, *, tm, tn):
    M, D = h.shape
    K = y.shape[1]
    ng = D // tn
    return pl.pallas_call(
        _glu_kernel,
        grid=(M // tm, ng),
        in_specs=[pl.BlockSpec((tm, K), lambda i, j: (i, 0)),
                  pl.BlockSpec((K, tn), lambda i, j: (0, j)),
                  pl.BlockSpec((K, tn), lambda i, j: (0, ng + j)),
                  pl.BlockSpec((tm, tn), lambda i, j: (i, j))],
        out_specs=pl.BlockSpec((tm, tn), lambda i, j: (i, j)),
        out_shape=jax.ShapeDtypeStruct((M, D), F32),
        compiler_params=_cparams(2),
        name="glu_proj",
    )(y, w_glu, w_glu, h)


def _tile(n, pref):
    t = min(n, pref)
    assert n % t == 0, (n, pref)
    return t


def _pack_in_proj(w_in):
    gw = GDN_HEADS * HEAD_DIM
    fw = FOX_HEADS * HEAD_DIM
    o = 4 * gw
    big = jnp.concatenate([w_in[:, :o], w_in[:, o + 2 * GDN_HEADS:o + 2 * GDN_HEADS + 3 * fw]], axis=1)
    small = jnp.concatenate([w_in[:, o:o + 2 * GDN_HEADS], w_in[:, o + 2 * GDN_HEADS + 3 * fw:]], axis=1)
    small = jnp.pad(small, ((0, 0), (0, LANES - small.shape[1])))
    return big.astype(BF16), small.astype(BF16)


def _gate_params(a_log, dt_bias, f_bias):
    prm = jnp.zeros((8, LANES), F32)
    prm = prm.at[0, SM_A:SM_A + GDN_HEADS].set(a_log.astype(F32))
    prm = prm.at[1, SM_A:SM_A + GDN_HEADS].set(dt_bias.astype(F32))
    prm = prm.at[2, SM_F:SM_F + FOX_HEADS].set(f_bias.astype(F32))
    return prm


def kernel(x, norm_mix, norm_mlp, norm_final, w_in, conv_qkv, gdn_a_log, gdn_dt_bias, gdn_norm, fox_f_bias, w_out, s5_lambda_re, s5_lambda_im, s5_b_re, s5_b_im, s5_c_re, s5_c_im, s5_d, s5_log_dt, w_glu, w_up, w_down):
    B, S, D = x.shape
    M = B * S
    depth = norm_mix.shape[0]
    h = x.reshape(M, D).astype(F32)

    tm = _tile(M, 512)
    tm_wide = _tile(M, 1024)
    tn = _tile(D, 512)
    tf = _tile(w_up.shape[2], 512)
    tp = _tile(S, 256)
    tg = _tile(S, 512)
    tq = _tile(S, 256)
    ts = _tile(S, 4096)
    ncb = ts // S5_L

    for layer in range(depth):
        i = layer // 2
        nw = norm_mix[layer].reshape(1, D).astype(F32)
        if layer % 2 == 0:
            wbig, wsmall = _pack_in_proj(w_in[i])
            big, small = norm_proj(h, nw, wbig, wsmall, tm=tm, tn=_tile(wbig.shape[1], 512))
            cw = conv_qkv[i].astype(F32).reshape(CONV_WIDTH, 3 * GDN_HEADS, LANES)
            prm = _gate_params(gdn_a_log[i], gdn_dt_bias[i], fox_f_bias[i])
            q, qg, k, kw, kd, vb, db, dc, el, cq, ck = gdn_prep(big, small, cw, prm, B=B, S=S, tp=tp)
            o_gdn = gdn_core(q, qg, k, kw, kd, vb, big, db, dc, el,
                             gdn_norm[i].reshape(1, HEAD_DIM).astype(F32), B=B, S=S, tg=tg)
            o_fox = fox_attention(big, cq, ck, B=B, S=S, tq=tq)
            h = out_proj(o_gdn, o_fox, w_out[i].astype(BF16), h, tm=tm_wide, tn=tn)
        else:
            u = rmsnorm(h, nw, tm=tm)
            ops = s5_operators(s5_lambda_re[i], s5_lambda_im[i], s5_b_re[i], s5_b_im[i],
                               s5_c_re[i], s5_c_im[i], s5_d[i], s5_log_dt[i], ncb)
            y = s5_core(u, *ops, B=B, S=S, ts=ts)
            h = glu_proj(y, w_glu[i].astype(BF16), h, tm=tm_wide, tn=tn)
        last = layer == depth - 1
        h = mlp(h, norm_mlp[layer].reshape(1, D).astype(F32), w_up[layer].astype(BF16),
                w_down[layer].astype(BF16), norm_final.reshape(1, D).astype(F32),
                tm=tm, tf=tf, final_norm=last)
    return h.reshape(B, S, D).astype(x.dtype)
```

```python
import functools
import math

import jax
import jax.numpy as jnp
from jax import lax
from jax.experimental import pallas as pl
from jax.experimental.pallas import tpu as pltpu

F32 = jnp.float32
BF16 = jnp.bfloat16

LANES = 128
HEAD_DIM = 128
GDN_HEADS = 16
FOX_HEADS = 16
CONV_WIDTH = 4
GDN_CHUNK = 64
GDN_HB = 4
S5_GROUP = 16
S5_STATE = 64
S5_L = 16
S5_GPT = LANES // S5_GROUP
RMS_EPS = 1e-6
L2_EPS = 1e-6
VMEM_LIMIT = 60 * 1024 * 1024

SM_BETA, SM_A, SM_F = 0, 16, 32


def _cparams(n_axes):
    return pltpu.CompilerParams(dimension_semantics=("arbitrary",) * n_axes,
                                vmem_limit_bytes=VMEM_LIMIT)


def _rms_rows(x, w):
    ms = jnp.mean(x * x, axis=-1, keepdims=True)
    return x * lax.rsqrt(ms + RMS_EPS) * w


def _norm_to(h_ref, nw_ref, dst_ref, rows):
    tm = h_ref.shape[0]

    def body(r, c):
        sl = pl.ds(pl.multiple_of(r * rows, rows), rows)
        dst_ref[sl, :] = _rms_rows(h_ref[sl, :], nw_ref[...]).astype(dst_ref.dtype)
        return c

    lax.fori_loop(0, tm // rows, body, 0)


def _norm_proj_kernel(h_ref, nw_ref, wb_ref, ws_ref, big_ref, small_ref, hn_ref, *, tn):
    @pl.when(pl.program_id(1) == 0)
    def _():
        _norm_to(h_ref, nw_ref, hn_ref, 32)
        small_ref[...] = jnp.dot(hn_ref[...], ws_ref[...], preferred_element_type=F32)

    acc = jnp.dot(hn_ref[...], wb_ref[0], preferred_element_type=F32)
    for c in range(tn // LANES):
        big_ref[c] = acc[:, c * LANES:(c + 1) * LANES].astype(BF16)


def _col_tiles(w, tn):
    K, N = w.shape
    return w.reshape(K, N // tn, tn).transpose(1, 0, 2)


def norm_proj(h, nw, wbig, wsmall, *, tm):
    M, D = h.shape
    tn = wbig.shape[2]
    N = wbig.shape[0] * tn
    return pl.pallas_call(
        functools.partial(_norm_proj_kernel, tn=tn),
        grid=(M // tm, N // tn),
        in_specs=[pl.BlockSpec((tm, D), lambda i, j: (i, 0)),
                  pl.BlockSpec((1, D), lambda i, j: (0, 0)),
                  pl.BlockSpec((1, D, tn), lambda i, j: (j, 0, 0)),
                  pl.BlockSpec((D, LANES), lambda i, j: (0, 0))],
        out_specs=[pl.BlockSpec((tn // LANES, tm, LANES), lambda i, j: (j, i, 0)),
                   pl.BlockSpec((tm, LANES), lambda i, j: (i, 0))],
        out_shape=[jax.ShapeDtypeStruct((N // LANES, M, LANES), BF16),
                   jax.ShapeDtypeStruct((M, LANES), F32)],
        scratch_shapes=[pltpu.VMEM((tm, D), BF16)],
        compiler_params=_cparams(2),
        name="norm_proj",
    )(h, nw, wbig, wsmall)


def _softplus(x):
    return jnp.maximum(x, 0.0) + jnp.log1p(jnp.exp(-jnp.abs(x)))


def _prep_kernel(qkv_ref, prev_ref, sm_ref, cw_ref, prm_ref,
                 q_ref, qg_ref, k_ref, kw_ref, kd_ref, vb_ref,
                 db_ref, dc_ref, el_ref, cq_ref, carry_ref, *, tp):
    t = pl.program_id(1)
    nh = GDN_HEADS
    nck = tp // GDN_CHUNK

    @pl.when(t == 0)
    def _():
        carry_ref[...] = jnp.zeros_like(carry_ref)

    sm = sm_ref[...]
    a_log = prm_ref[0:1, :]
    dt_bias = prm_ref[1:2, :]
    f_bias = prm_ref[2:3, :]
    beta = jax.nn.sigmoid(sm)
    g = -jnp.exp(a_log) * _softplus(sm + dt_bias)
    logf = -_softplus(-(sm + f_bias))

    row = lax.broadcasted_iota(jnp.int32, (tp, tp), 0)
    col = lax.broadcasted_iota(jnp.int32, (tp, tp), 1)
    tri_full = (row >= col).astype(F32)
    csh = GDN_CHUNK.bit_length() - 1
    tri_chunk = jnp.where((row >> csh) == (col >> csh), tri_full, 0.0)
    hi = lax.Precision.HIGHEST
    gcum = jnp.dot(tri_chunk, g, precision=hi, preferred_element_type=F32)
    ccum = jnp.dot(tri_full, logf, precision=hi, preferred_element_type=F32) + carry_ref[0:1, :]
    carry_ref[0:1, :] = ccum[tp - 1:tp, :]

    glast = jnp.concatenate(
        [jnp.broadcast_to(gcum[(c + 1) * GDN_CHUNK - 1:(c + 1) * GDN_CHUNK, :], (GDN_CHUNK, LANES))
         for c in range(nck)], axis=0)
    exp_g = jnp.exp(gcum)
    exp_kd = jnp.exp(glast - gcum)
    exp_last = jnp.exp(glast)

    comb_t = gcum.T

    ntile = qkv_ref.shape[0]
    prev_ok = (t > 0).astype(F32)
    ci = lax.broadcasted_iota(jnp.int32, (GDN_CHUNK, GDN_CHUNK), 0)
    cj = lax.broadcasted_iota(jnp.int32, (GDN_CHUNK, GDN_CHUNK), 1)

    def conv_tile(idx):
        cur = qkv_ref[idx].astype(F32)
        tail = prev_ref[idx].astype(F32)[8:16, :] * prev_ok
        xc = jnp.concatenate([tail, cur], axis=0)
        w = cw_ref[:, idx, :]
        acc = xc[8:8 + tp, :] * w[3:4, :]
        for j in range(CONV_WIDTH - 1):
            sh = CONV_WIDTH - 1 - j
            acc = acc + xc[8 - sh:8 - sh + tp, :] * w[j:j + 1, :]
        return acc * jax.nn.sigmoid(acc)

    def l2n(x):
        return x * lax.rsqrt(jnp.sum(x * x, axis=-1, keepdims=True) + L2_EPS)

    scale = HEAD_DIM ** -0.5
    for h in range(nh):
        qh = l2n(conv_tile(h)) * scale
        kh = l2n(conv_tile(nh + h))
        vh = conv_tile(2 * nh + h)
        b_col = beta[:, SM_BETA + h:SM_BETA + h + 1]
        eg_col = exp_g[:, SM_A + h:SM_A + h + 1]
        ekd_col = exp_kd[:, SM_A + h:SM_A + h + 1]
        q_ref[h] = qh.astype(BF16)
        qg_ref[h] = (qh * eg_col).astype(BF16)
        k_ref[h] = kh.astype(BF16)
        kw_ref[h] = (kh * (b_col * eg_col)).astype(BF16)
        kd_ref[h] = (kh * ekd_col).astype(BF16)
        vb_ref[h] = (vh * b_col).astype(BF16)
        g_col = gcum[:, SM_A + h:SM_A + h + 1]
        for c in range(nck):
            rs = slice(c * GDN_CHUNK, (c + 1) * GDN_CHUNK)
            diff = g_col[rs, :] - comb_t[SM_A + h:SM_A + h + 1, rs]
            dec = jnp.exp(jnp.where(ci >= cj, diff, -jnp.inf))
            dc_ref[0, h, c] = dec
            db_ref[0, h, c] = jnp.where(ci > cj, dec * b_col[rs, :], 0.0)
            el_ref[0, h, c] = exp_last[c * GDN_CHUNK:c * GDN_CHUNK + 1, SM_A + h:SM_A + h + 1] * jnp.ones((1, LANES), F32)
        cq_ref[0, h] = jnp.broadcast_to(ccum[:, SM_F + h:SM_F + h + 1], (tp, LANES))


def gdn_prep(big, small, conv_w, prm, *, B, S, tp):
    M = B * S
    nt = S // tp
    nh = GDN_HEADS
    nqkv = 3 * nh
    nck = tp // GDN_CHUNK
    NC = S // GDN_CHUNK
    kern = functools.partial(_prep_kernel, tp=tp)
    head_spec = pl.BlockSpec((nh, tp, LANES), lambda b, t: (0, b * nt + t, 0))
    head_shape = jax.ShapeDtypeStruct((nh, M, LANES), BF16)
    return pl.pallas_call(
        kern,
        grid=(B, nt),
        in_specs=[pl.BlockSpec((nqkv, tp, LANES), lambda b, t: (0, b * nt + t, 0)),
                  pl.BlockSpec((nqkv, 16, LANES),
                               lambda b, t: (0, jnp.maximum((b * nt + t) * (tp // 16) - 1, 0), 0)),
                  pl.BlockSpec((tp, LANES), lambda b, t: (b * nt + t, 0)),
                  pl.BlockSpec((CONV_WIDTH, nqkv, LANES), lambda b, t: (0, 0, 0)),
                  pl.BlockSpec((8, LANES), lambda b, t: (0, 0))],
        out_specs=[head_spec] * 6 + [
            pl.BlockSpec((1, nh, nck, GDN_CHUNK, GDN_CHUNK), lambda b, t: (b, 0, t, 0, 0)),
            pl.BlockSpec((1, nh, nck, GDN_CHUNK, GDN_CHUNK), lambda b, t: (b, 0, t, 0, 0)),
            pl.BlockSpec((1, nh, nck, 1, LANES), lambda b, t: (b, 0, t, 0, 0)),
            pl.BlockSpec((1, nh, tp, LANES), lambda b, t: (b, 0, t, 0)),
        ],
        out_shape=[head_shape] * 6 + [
            jax.ShapeDtypeStruct((B, nh, NC, GDN_CHUNK, GDN_CHUNK), F32),
            jax.ShapeDtypeStruct((B, nh, NC, GDN_CHUNK, GDN_CHUNK), F32),
            jax.ShapeDtypeStruct((B, nh, NC, 1, LANES), F32),
            jax.ShapeDtypeStruct((B, nh, S, LANES), F32),
        ],
        scratch_shapes=[pltpu.VMEM((8, LANES), F32)],
        compiler_params=_cparams(2),
        name="gdn_prep",
    )(big, big, small, conv_w, prm)


def _bdot(a, b):
    return lax.dot_general(a, b, (((2,), (1,)), ((0,), (0,))), preferred_element_type=F32)


def _gdn_kernel(q_ref, qg_ref, k_ref, kw_ref, kd_ref, vb_ref, z_ref, db_ref, dc_ref, el_ref, nw_ref,
                o_ref, s_ref, *, tg, hb):
    C = GDN_CHUNK
    nck = tg // C

    @pl.when(pl.program_id(2) == 0)
    def _():
        s_ref[...] = jnp.zeros_like(s_ref)

    ii = lax.broadcasted_iota(jnp.int32, (C, C), 0)
    jj = lax.broadcasted_iota(jnp.int32, (C, C), 1)
    eye = (ii == jj).astype(F32)
    us, ws, attns = [], [], []
    for hh in range(hb):
        k3 = k_ref[hh].reshape(nck, C, HEAD_DIM)
        q3 = q_ref[hh].reshape(nck, C, HEAD_DIM)
        kq = jnp.concatenate([k3, q3], axis=1)
        p = lax.dot_general(kq, k3, (((2,), (2,)), ((0,), (0,))), preferred_element_type=F32)
        m = p[:, :C, :] * db_ref[0, hh]
        attns.append((p[:, C:, :] * dc_ref[0, hh]).astype(BF16))

        x = jnp.broadcast_to(eye, (nck, C, C))
        s = 1
        while s < C:
            ls = s.bit_length() - 1
            sel = ((ii >> (ls + 1)) == (jj >> (ls + 1))) & (((ii >> ls) & 1) == 1) & (((jj >> ls) & 1) == 0)
            cs = jnp.where(sel, m, 0.0).astype(BF16)
            xb = x.astype(BF16)
            x = x - _bdot(_bdot(xb, cs).astype(BF16), xb)
            s *= 2
        xb = x.astype(BF16)
        us.append(_bdot(xb, vb_ref[hh].reshape(nck, C, HEAD_DIM)))
        ws.append(_bdot(xb, kw_ref[hh].reshape(nck, C, HEAD_DIM)).astype(BF16))

    nw = nw_ref[...]
    sts = [s_ref[hh] for hh in range(hb)]
    for c in range(nck):
        rs = slice(c * C, (c + 1) * C)
        for hh in range(hb):
            sb = sts[hh].astype(BF16)
            wq = jnp.concatenate([ws[hh][c], qg_ref[hh, rs, :]], axis=0)
            r1 = jnp.dot(wq, sb, preferred_element_type=F32)
            v_new = us[hh][c] - r1[:C, :]
            vb16 = v_new.astype(BF16)
            o = r1[C:, :] + jnp.dot(attns[hh][c], vb16, preferred_element_type=F32)
            sts[hh] = sts[hh] * el_ref[0, hh, c] + lax.dot_general(
                kd_ref[hh, rs, :], vb16, (((0,), (0,)), ((), ())), preferred_element_type=F32)
            z = z_ref[hh, rs, :].astype(F32)
            o = _rms_rows(o, nw) * (z * jax.nn.sigmoid(z))
            o_ref[rs, hh * HEAD_DIM:(hh + 1) * HEAD_DIM] = o.astype(BF16)
    for hh in range(hb):
        s_ref[hh] = sts[hh]


def gdn_core(q, qg, k, kw, kd, vb, big, db, dc, el, gdn_nw, *, B, S, tg, hb):
    M = B * S
    nt = S // tg
    nh = GDN_HEADS
    nck = tg // GDN_CHUNK
    z_blk0 = 3 * nh // hb
    assert nh % hb == 0 and (3 * nh) % hb == 0
    hs = pl.BlockSpec((hb, tg, LANES), lambda b, h, t: (h, b * nt + t, 0))
    ds_ = pl.BlockSpec((1, hb, nck, GDN_CHUNK, GDN_CHUNK), lambda b, h, t: (b, h, t, 0, 0))
    return pl.pallas_call(
        functools.partial(_gdn_kernel, tg=tg, hb=hb),
        grid=(B, nh // hb, nt),
        in_specs=[hs] * 6 + [
            pl.BlockSpec((hb, tg, LANES), lambda b, h, t: (z_blk0 + h, b * nt + t, 0)),
            ds_, ds_,
            pl.BlockSpec((1, hb, nck, 1, LANES), lambda b, h, t: (b, h, t, 0, 0)),
            pl.BlockSpec((1, LANES), lambda b, h, t: (0, 0))],
        out_specs=pl.BlockSpec((tg, hb * HEAD_DIM), lambda b, h, t: (b * nt + t, h)),
        out_shape=jax.ShapeDtypeStruct((M, nh * HEAD_DIM), BF16),
        scratch_shapes=[pltpu.VMEM((hb, HEAD_DIM, HEAD_DIM), F32)],
        compiler_params=_cparams(3),
        name="gdn_core",
    )(q, qg, k, kw, kd, vb, big, db, dc, el, gdn_nw)


LOG2E = 1.4426950408889634


def _split3(c):
    hi = c.astype(BF16).astype(F32)
    r1 = c - hi
    mid = r1.astype(BF16).astype(F32)
    lo = (r1 - mid).astype(BF16).astype(F32)
    return hi, mid, lo


def _fox_aug(c, sign_c, ones_first):
    lane = lax.broadcasted_iota(jnp.int32, c.shape, 1)
    hi, mid, lo = _split3(c)
    k3 = lane - 3 * (lane >= 3).astype(jnp.int32)
    cpart = jnp.where(k3 == 0, hi, jnp.where(k3 == 1, mid, lo)) * sign_c
    c_lanes = (lane >= 3) if ones_first else (lane < 3)
    a = jnp.where(lane < 6, jnp.where(c_lanes, cpart, 1.0), 0.0)
    return a.astype(BF16)


def _fox_kernel(q_ref, k_ref, v_ref, c_ref, o_ref, ka_ref, va_ref, qa_ref, m_ref, acc_ref, *, tq, tk, S):
    i = pl.program_id(2)
    rows = min(256, S)

    @pl.when(i == 0)
    def _():
        def body(r, carry):
            sl = pl.ds(pl.multiple_of(r * rows, rows), rows)
            ka_ref[sl, :LANES] = k_ref[0, sl, :]
            ka_ref[sl, LANES:] = _fox_aug(c_ref[0, 0, sl, :] * LOG2E, -1.0, True)
            va_ref[sl, :LANES] = v_ref[0, sl, :]
            va_ref[sl, LANES:] = jnp.ones((rows, LANES), BF16)
            return carry
        lax.fori_loop(0, S // rows, body, 0)

    q_rows = pl.ds(pl.multiple_of(i * tq, tq), tq)
    qa_ref[:, :LANES] = (q_ref[0].astype(F32) * (HEAD_DIM ** -0.5 * LOG2E)).astype(BF16)
    qa_ref[:, LANES:] = _fox_aug(c_ref[0, 0, q_rows, :] * LOG2E, 1.0, False)
    m_ref[...] = jnp.full(m_ref.shape, -jnp.inf, F32)
    acc_ref[...] = jnp.zeros(acc_ref.shape, F32)

    def block(j, r0, masked):
        nr = tq - r0
        ks = pl.ds(pl.multiple_of(j * tk, tk), tk)
        s = lax.dot_general(qa_ref[r0:, :], ka_ref[ks, :], (((1,), (1,)), ((), ())),
                            preferred_element_type=F32)
        if masked:
            r = lax.broadcasted_iota(jnp.int32, (nr, tk), 0) + (i * tq + r0)
            c = lax.broadcasted_iota(jnp.int32, (nr, tk), 1) + j * tk
            s = jnp.where(c <= r, s, -jnp.inf)
        m_prev = m_ref[r0:, :]
        m_new = jnp.maximum(m_prev, jnp.max(s, axis=-1, keepdims=True))
        m_ref[r0:, :] = m_new
        p = jnp.exp2(s - jnp.concatenate([m_new] * (tk // LANES), axis=1)).astype(BF16)
        alpha = jnp.exp2(m_prev - m_new)
        acc_ref[r0:, :] = (jnp.concatenate([alpha, alpha], axis=1) * acc_ref[r0:, :]
                           + jnp.dot(p, va_ref[ks, :], preferred_element_type=F32))

    npb = tq // tk

    def full_body(jb, carry):
        for d in range(npb):
            block(jb * npb + d, 0, False)
        return carry

    lax.fori_loop(0, i, full_body, 0)
    for d in range(npb):
        block(i * npb + d, d * tk, True)
    acc = acc_ref[...]
    o_ref[...] = (acc[:, :HEAD_DIM] / acc[:, HEAD_DIM:]).astype(BF16)


def fox_attention(big, c, *, B, S, tq, tk):
    M = B * S
    nq = S // tq
    nh = FOX_HEADS
    q0 = 4 * GDN_HEADS
    k0 = q0 + nh
    v0 = k0 + nh
    return pl.pallas_call(
        functools.partial(_fox_kernel, tq=tq, tk=tk, S=S),
        grid=(B, nh, nq),
        in_specs=[pl.BlockSpec((1, tq, LANES), lambda b, h, i: (q0 + h, b * nq + i, 0)),
                  pl.BlockSpec((1, S, LANES), lambda b, h, i: (k0 + h, b, 0)),
                  pl.BlockSpec((1, S, LANES), lambda b, h, i: (v0 + h, b, 0)),
                  pl.BlockSpec((1, 1, S, LANES), lambda b, h, i: (b, h, 0, 0))],
        out_specs=pl.BlockSpec((tq, LANES), lambda b, h, i: (b * nq + i, h)),
        out_shape=jax.ShapeDtypeStruct((M, nh * HEAD_DIM), BF16),
        scratch_shapes=[pltpu.VMEM((S, 2 * LANES), BF16),
                        pltpu.VMEM((S, 2 * LANES), BF16),
                        pltpu.VMEM((tq, 2 * LANES), BF16),
                        pltpu.VMEM((tq, LANES), F32),
                        pltpu.VMEM((tq, 2 * HEAD_DIM), F32)],
        compiler_params=_cparams(3),
        name="fox_attention",
    )(big, big, big, c)


def _out_proj_kernel(og_ref, of_ref, w1_ref, w2_ref, h_ref, o_ref):
    acc = jnp.dot(og_ref[...], w1_ref[0, 0], preferred_element_type=F32)
    acc = acc + jnp.dot(of_ref[...], w2_ref[0, 0], preferred_element_type=F32)
    o_ref[...] = h_ref[...] + acc


def out_proj(og, of, w_out, h, *, tm):
    M, D = h.shape
    Kg = og.shape[1]
    tn = w_out.shape[3]
    return pl.pallas_call(
        _out_proj_kernel,
        grid=(M // tm, D // tn),
        in_specs=[pl.BlockSpec((tm, Kg), lambda i, j: (i, 0)),
                  pl.BlockSpec((tm, Kg), lambda i, j: (i, 0)),
                  pl.BlockSpec((1, 1, Kg, tn), lambda i, j: (0, j, 0, 0)),
                  pl.BlockSpec((1, 1, Kg, tn), lambda i, j: (1, j, 0, 0)),
                  pl.BlockSpec((tm, tn), lambda i, j: (i, j))],
        out_specs=pl.BlockSpec((tm, tn), lambda i, j: (i, j)),
        out_shape=jax.ShapeDtypeStruct((M, D), F32),
        compiler_params=_cparams(2),
        name="out_proj",
    )(og, of, w_out, w_out, h)


def _mlp_kernel(h_ref, nw_ref, wu_ref, wd_ref, fw_ref, o_ref, hn_ref, *, final_norm):
    f = pl.program_id(1)

    @pl.when(f == 0)
    def _():
        _norm_to(h_ref, nw_ref, hn_ref, 32)
        o_ref[...] = h_ref[...]

    a = jnp.dot(hn_ref[...], wu_ref[0], preferred_element_type=F32)
    a = jnp.maximum(a, 0.0)
    a = (a * a).astype(BF16)
    o_ref[...] += jnp.dot(a, wd_ref[...], preferred_element_type=F32)

    if final_norm:
        @pl.when(f == pl.num_programs(1) - 1)
        def _():
            _norm_to(o_ref, fw_ref, o_ref, 32)


def mlp(h, nw, w_up, w_down, fw, *, tm, final_norm):
    M, D = h.shape
    tf = w_up.shape[2]
    F = w_up.shape[0] * tf
    return pl.pallas_call(
        functools.partial(_mlp_kernel, final_norm=final_norm),
        grid=(M // tm, F // tf),
        in_specs=[pl.BlockSpec((tm, D), lambda i, f: (i, 0)),
                  pl.BlockSpec((1, D), lambda i, f: (0, 0)),
                  pl.BlockSpec((1, D, tf), lambda i, f: (f, 0, 0)),
                  pl.BlockSpec((tf, D), lambda i, f: (f, 0)),
                  pl.BlockSpec((1, D), lambda i, f: (0, 0))],
        out_specs=pl.BlockSpec((tm, D), lambda i, f: (i, 0)),
        out_shape=jax.ShapeDtypeStruct((M, D), F32),
        scratch_shapes=[pltpu.VMEM((tm, D), BF16)],
        compiler_params=_cparams(2),
        name="mlp",
    )(h, nw, w_up, w_down, fw)


def _rmsnorm_kernel(h_ref, nw_ref, o_ref):
    _norm_to(h_ref, nw_ref, o_ref, 32)


def rmsnorm(h, nw, *, tm):
    M, D = h.shape
    return pl.pallas_call(
        _rmsnorm_kernel,
        grid=(M // tm,),
        in_specs=[pl.BlockSpec((tm, D), lambda i: (i, 0)),
                  pl.BlockSpec((1, D), lambda i: (0, 0))],
        out_specs=pl.BlockSpec((tm, D), lambda i: (i, 0)),
        out_shape=jax.ShapeDtypeStruct((M, D), F32),
        compiler_params=_cparams(1),
        name="rmsnorm",
    )(h, nw)


def _s5_assemble(bd_ref, pe_ref, po_ref, q_ref, wt_ref, wi_ref, wo_ref):
    L = S5_L
    P = S5_STATE
    half = LANES // 2
    zero_tile = jnp.zeros((LANES, LANES), BF16)
    for s in range(L):
        for t in range(L):
            wt_ref[s * LANES:(s + 1) * LANES, t * LANES:(t + 1) * LANES] = (
                bd_ref[0, t - s] if t >= s else zero_tile)
    rgrp = lax.broadcasted_iota(jnp.int32, (LANES, LANES), 0) >> 4
    lane = lax.broadcasted_iota(jnp.int32, (LANES, LANES), 1)
    for s in range(L):
        pe = pe_ref[0, s]
        po = po_ref[0, s]
        for r in range(2):
            lo_src, hi_src = (pe, po) if r == 0 else (po, pe)
            for qd in range(S5_GPT // 2):
                tile = jnp.where((rgrp == 2 * qd) & (lane < half), lo_src,
                                 jnp.where((rgrp == 2 * qd + 1) & (lane >= half), hi_src, zero_tile))
                c0 = (r * (S5_GPT // 2) + qd) * LANES
                wi_ref[s * LANES:(s + 1) * LANES, c0:c0 + LANES] = tile
    lgrp = lax.broadcasted_iota(jnp.int32, (P, LANES), 1) >> 4
    zero_rows = jnp.zeros((P, LANES), BF16)
    for r in range(2):
        for t in range(L):
            src = q_ref[0, r, t]
            for h in range(S5_GPT):
                r0 = (r * S5_GPT + h) * P
                wo_ref[r0:r0 + P, t * LANES:(t + 1) * LANES] = jnp.where(lgrp == h, src, zero_rows)


def _s5_kernel(u_ref, bd_ref, pe_ref, po_ref, q_ref, ap_ref, d_ref, y_ref,
               wt_ref, wi_ref, wo_ref, x_ref, yn_ref, carry_ref, *, ts):
    L = S5_L
    ncb = ts // L
    nst = S5_GPT * S5_STATE

    @pl.when((pl.program_id(1) == 0) & (pl.program_id(2) == 0))
    def _():
        _s5_assemble(bd_ref, pe_ref, po_ref, q_ref, wt_ref, wi_ref, wo_ref)

    @pl.when(pl.program_id(2) == 0)
    def _():
        carry_ref[...] = jnp.zeros_like(carry_ref)

    for s in range(L):
        x_ref[:, s * LANES:(s + 1) * LANES] = u_ref[pl.ds(s, ncb, stride=L), :].astype(BF16)
    xb = x_ref[...]
    y_intra = jnp.dot(xb, wt_ref[...], preferred_element_type=F32)
    v = jnp.dot(xb, wi_ref[...], preferred_element_type=F32)

    rows = lax.broadcasted_iota(jnp.int32, (ncb, nst), 0)
    vr, vi = v[:, :nst], v[:, nst:]
    er = jnp.where(rows == 0, carry_ref[0:1, :], pltpu.roll(vr, 1, axis=0))
    ei = jnp.where(rows == 0, carry_ref[1:2, :], pltpu.roll(vi, 1, axis=0))
    d = 1
    k = 1
    while d < ncb:
        ar = ap_ref[0, k:k + 1, :nst]
        ai = ap_ref[0, k:k + 1, nst:]
        sr = jnp.where(rows >= d, pltpu.roll(er, d, axis=0), 0.0)
        si = jnp.where(rows >= d, pltpu.roll(ei, d, axis=0), 0.0)
        er, ei = er + ar * sr - ai * si, ei + ar * si + ai * sr
        d *= 2
        k += 1
    a1r = ap_ref[0, 0:1, :nst]
    a1i = ap_ref[0, 0:1, nst:]
    lr, li = er[ncb - 1:ncb, :], ei[ncb - 1:ncb, :]
    carry_ref[0:1, :] = a1r * lr - a1i * li + vr[ncb - 1:ncb, :]
    carry_ref[1:2, :] = a1r * li + a1i * lr + vi[ncb - 1:ncb, :]

    hprev = jnp.concatenate([er, ei], axis=1).astype(BF16)
    y = y_intra + jnp.dot(hprev, wo_ref[...], preferred_element_type=F32)
    dsk = d_ref[0]
    for s in range(L):
        rs = pl.ds(s, ncb, stride=L)
        yn_ref[rs, :] = y[:, s * LANES:(s + 1) * LANES] + dsk * u_ref[rs, :]
    yn = yn_ref[...]
    c0 = math.sqrt(2.0 / math.pi)
    y_ref[...] = (0.5 * yn * (1.0 + jnp.tanh(c0 * (yn + 0.044715 * (yn * yn * yn))))).astype(BF16)


def s5_core(u, bd, pe, po, qo, apow, dskip, *, B, S, ts):
    M, D = u.shape
    nj = D // LANES
    ns = S // ts
    ncb = ts // S5_L
    nst = S5_GPT * S5_STATE
    lw = S5_L * LANES
    return pl.pallas_call(
        functools.partial(_s5_kernel, ts=ts),
        grid=(nj, B, ns),
        in_specs=[pl.BlockSpec((ts, LANES), lambda j, b, s: (b * ns + s, j)),
                  pl.BlockSpec((1, S5_L, LANES, LANES), lambda j, b, s: (j, 0, 0, 0)),
                  pl.BlockSpec((1, S5_L, LANES, LANES), lambda j, b, s: (j, 0, 0, 0)),
                  pl.BlockSpec((1, S5_L, LANES, LANES), lambda j, b, s: (j, 0, 0, 0)),
                  pl.BlockSpec((1, 2, S5_L, S5_STATE, LANES), lambda j, b, s: (j, 0, 0, 0, 0)),
                  pl.BlockSpec((1, apow.shape[1], 2 * nst), lambda j, b, s: (j, 0, 0)),
                  pl.BlockSpec((1, 1, LANES), lambda j, b, s: (j, 0, 0))],
        out_specs=pl.BlockSpec((ts, LANES), lambda j, b, s: (b * ns + s, j)),
        out_shape=jax.ShapeDtypeStruct((M, D), BF16),
        scratch_shapes=[pltpu.VMEM((lw, lw), BF16),
                        pltpu.VMEM((lw, 2 * nst), BF16),
                        pltpu.VMEM((2 * nst, lw), BF16),
                        pltpu.VMEM((ncb, lw), BF16),
                        pltpu.VMEM((ts, LANES), F32),
                        pltpu.VMEM((8, nst), F32)],
        compiler_params=_cparams(3),
        name="s5_core",
    )(u, bd, pe, po, qo, apow, dskip)


def s5_operators(lam_re, lam_im, b_re, b_im, c_re, c_im, d_skip, log_dt, ncb):
    G, P = lam_re.shape
    L = S5_L
    gc = S5_GROUP
    lam = lax.complex(jnp.minimum(lam_re.astype(F32), -1e-4), lam_im.astype(F32))
    dt = jnp.exp(log_dt.astype(F32))[:, None]
    lam_dt = lam * dt
    lam_bar = jnp.exp(lam_dt)
    b_bar = ((lam_bar - 1.0) / lam)[:, :, None] * lax.complex(b_re.astype(F32), b_im.astype(F32))
    c = lax.complex(c_re.astype(F32), c_im.astype(F32))
    kk = jnp.arange(L + 1, dtype=F32)
    pows = jnp.exp(lam_dt[None] * kk[:, None, None])

    nj = G // S5_GPT
    kern = jnp.einsum("gcp,kgp,gpd->kgcd", c, pows[:L], b_bar).real
    eye = jnp.eye(S5_GPT, dtype=F32)
    bd = jnp.einsum("kjgcd,gh->jkgdhc", kern.reshape(L, nj, S5_GPT, gc, gc), eye)
    bd = bd.reshape(nj, L, LANES, LANES).astype(BF16)

    sidx = jnp.arange(L)
    min_c = pows[L - 1 - sidx][:, :, :, None] * b_bar[None]
    min_c = min_c.transpose(1, 0, 3, 2).reshape(nj, S5_GPT, L, gc, P).transpose(0, 2, 1, 3, 4)
    pe = jnp.concatenate([min_c.real, min_c.imag], axis=-1).reshape(nj, L, LANES, 2 * P).astype(BF16)
    po = jnp.concatenate([min_c.imag, min_c.real], axis=-1).reshape(nj, L, LANES, 2 * P).astype(BF16)

    mout_c = c[None] * pows[1:L + 1][:, :, None, :]
    mout_c = mout_c.reshape(L, nj, S5_GPT, gc, P).transpose(1, 0, 4, 2, 3)
    qo = jnp.stack([mout_c.real, -mout_c.imag], axis=1).reshape(nj, 2, L, P, LANES).astype(BF16)

    nsteps = max(1, int(math.log2(ncb)))
    mult = jnp.asarray([1.0] + [2.0 ** k for k in range(nsteps)], F32) * L
    ap = jnp.exp(lam_dt[None] * mult[:, None, None])
    ap = jnp.stack([ap.real, ap.imag], axis=1).reshape(1 + nsteps, 2, nj, S5_GPT * P)
    ap = ap.transpose(2, 0, 1, 3).reshape(nj, 1 + nsteps, 2 * S5_GPT * P)
    pad = (-ap.shape[1]) % 8
    ap = jnp.pad(ap, ((0, 0), (0, pad), (0, 0)))
    dsk = d_skip.astype(F32).reshape(nj, 1, LANES)
    return bd, pe, po, qo, ap, dsk


def _glu_kernel(y_ref, wv_ref, wg_ref, h_ref, o_ref):
    y = y_ref[...]
    val = jnp.dot(y, wv_ref[0], preferred_element_type=F32)
    gate = jnp.dot(y, wg_ref[0], preferred_element_type=F32)
    o_ref[...] = h_ref[...] + val * jax.nn.sigmoid(gate)


def glu_proj(y, w_glu, h, *, tm):
    M, D = h.shape
    K = y.shape[1]
    tn = w_glu.shape[2]
    ng = D // tn
    return pl.pallas_call(
        _glu_kernel,
        grid=(M // tm, ng),
        in_specs=[pl.BlockSpec((tm, K), lambda i, j: (i, 0)),
                  pl.BlockSpec((1, K, tn), lambda i, j: (j, 0, 0)),
                  pl.BlockSpec((1, K, tn), lambda i, j: (ng + j, 0, 0)),
                  pl.BlockSpec((tm, tn), lambda i, j: (i, j))],
        out_specs=pl.BlockSpec((tm, tn), lambda i, j: (i, j)),
        out_shape=jax.ShapeDtypeStruct((M, D), F32),
        compiler_params=_cparams(2),
        name="glu_proj",
    )(y, w_glu, w_glu, h)


def _tile(n, pref):
    t = min(n, pref)
    assert n % t == 0, (n, pref)
    return t


def _pack_in_proj(w_in):
    gw = GDN_HEADS * HEAD_DIM
    fw = FOX_HEADS * HEAD_DIM
    o = 4 * gw
    big = jnp.concatenate([w_in[:, :o], w_in[:, o + 2 * GDN_HEADS:o + 2 * GDN_HEADS + 3 * fw]], axis=1)
    small = jnp.concatenate([w_in[:, o:o + 2 * GDN_HEADS], w_in[:, o + 2 * GDN_HEADS + 3 * fw:]], axis=1)
    small = jnp.pad(small, ((0, 0), (0, LANES - small.shape[1])))
    return big.astype(BF16), small.astype(BF16)


def _gate_params(a_log, dt_bias, f_bias):
    prm = jnp.zeros((8, LANES), F32)
    prm = prm.at[0, SM_A:SM_A + GDN_HEADS].set(a_log.astype(F32))
    prm = prm.at[1, SM_A:SM_A + GDN_HEADS].set(dt_bias.astype(F32))
    prm = prm.at[2, SM_F:SM_F + FOX_HEADS].set(f_bias.astype(F32))
    return prm


def kernel(x, norm_mix, norm_mlp, norm_final, w_in, conv_qkv, gdn_a_log, gdn_dt_bias, gdn_norm, fox_f_bias, w_out, s5_lambda_re, s5_lambda_im, s5_b_re, s5_b_im, s5_c_re, s5_c_im, s5_d, s5_log_dt, w_glu, w_up, w_down):
    B, S, D = x.shape
    M = B * S
    depth = norm_mix.shape[0]
    h = x.reshape(M, D).astype(F32)

    tm = _tile(M, 512)
    tm_wide = _tile(M, 1024)
    tn = _tile(D, 512)
    tf = _tile(w_up.shape[2], 512)
    tp = _tile(S, 256)
    tg = _tile(S, 512)
    tq = _tile(S, 1024)
    tk = _tile(tq, 256)
    ts = _tile(S, 4096)
    ncb = ts // S5_L
    kg = GDN_HEADS * HEAD_DIM

    for layer in range(depth):
        i = layer // 2
        nw = norm_mix[layer].reshape(1, D).astype(F32)
        if layer % 2 == 0:
            wbig, wsmall = _pack_in_proj(w_in[i])
            wbig = _col_tiles(wbig, _tile(wbig.shape[1], 512))
            big, small = norm_proj(h, nw, wbig, wsmall, tm=tm)
            cw = conv_qkv[i].astype(F32).reshape(CONV_WIDTH, 3 * GDN_HEADS, LANES)
            prm = _gate_params(gdn_a_log[i], gdn_dt_bias[i], fox_f_bias[i])
            q, qg, k, kw, kd, vb, db, dc, el, cf = gdn_prep(big, small, cw, prm, B=B, S=S, tp=tp)
            o_gdn = gdn_core(q, qg, k, kw, kd, vb, big, db, dc, el,
                             gdn_norm[i].reshape(1, HEAD_DIM).astype(F32), B=B, S=S, tg=tg, hb=GDN_HB)
            o_fox = fox_attention(big, cf, B=B, S=S, tq=tq, tk=tk)
            wo = w_out[i].astype(BF16).reshape(2, kg, D // tn, tn).transpose(0, 2, 1, 3)
            h = out_proj(o_gdn, o_fox, wo, h, tm=tm_wide)
        else:
            u = rmsnorm(h, nw, tm=tm)
            ops = s5_operators(s5_lambda_re[i], s5_lambda_im[i], s5_b_re[i], s5_b_im[i],
                               s5_c_re[i], s5_c_im[i], s5_d[i], s5_log_dt[i], ncb)
            y = s5_core(u, *ops, B=B, S=S, ts=ts)
            h = glu_proj(y, _col_tiles(w_glu[i].astype(BF16), tn), h, tm=tm_wide)
        last = layer == depth - 1
        h = mlp(h, norm_mlp[layer].reshape(1, D).astype(F32), _col_tiles(w_up[layer].astype(BF16), tf),
                w_down[layer].astype(BF16), norm_final.reshape(1, D).astype(F32),
                tm=tm, final_norm=last)
    return h.reshape(B, S, D).astype(x.dtype)
```

```python
import functools
import math

import jax
import jax.numpy as jnp
from jax import lax
from jax.experimental import pallas as pl
from jax.experimental.pallas import tpu as pltpu

F32 = jnp.float32
BF16 = jnp.bfloat16

LANES = 128
HEAD_DIM = 128
GDN_HEADS = 16
FOX_HEADS = 16
CONV_WIDTH = 4
GDN_CHUNK = 64
GDN_HB = 8
GDN_PACK = 4
S5_GROUP = 16
S5_STATE = 64
S5_L = 16
S5_GPT = LANES // S5_GROUP
RMS_EPS = 1e-6
L2_EPS = 1e-6
VMEM_LIMIT = 60 * 1024 * 1024

SM_BETA, SM_A, SM_F = 0, 16, 32


def _cparams(n_axes):
    return pltpu.CompilerParams(dimension_semantics=("arbitrary",) * n_axes,
                                vmem_limit_bytes=VMEM_LIMIT)


def _rms_rows(x, w):
    ms = jnp.mean(x * x, axis=-1, keepdims=True)
    return x * lax.rsqrt(ms + RMS_EPS) * w


def _norm_to(h_ref, nw_ref, dst_ref, rows):
    tm = h_ref.shape[0]

    def body(r, c):
        sl = pl.ds(pl.multiple_of(r * rows, rows), rows)
        dst_ref[sl, :] = _rms_rows(h_ref[sl, :], nw_ref[...]).astype(dst_ref.dtype)
        return c

    lax.fori_loop(0, tm // rows, body, 0)


def _norm_proj_kernel(h_ref, nw_ref, wb_ref, ws_ref, big_ref, small_ref, hn_ref, *, tn):
    @pl.when(pl.program_id(1) == 0)
    def _():
        _norm_to(h_ref, nw_ref, hn_ref, 32)
        small_ref[...] = jnp.dot(hn_ref[...], ws_ref[...], preferred_element_type=F32)

    acc = jnp.dot(hn_ref[...], wb_ref[...], preferred_element_type=F32)
    for c in range(tn // LANES):
        big_ref[c] = acc[:, c * LANES:(c + 1) * LANES].astype(BF16)


def norm_proj(h, nw, wbig, wsmall, *, tm, tn):
    M, D = h.shape
    N = wbig.shape[1]
    return pl.pallas_call(
        functools.partial(_norm_proj_kernel, tn=tn),
        grid=(M // tm, N // tn),
        in_specs=[pl.BlockSpec((tm, D), lambda i, j: (i, 0)),
                  pl.BlockSpec((1, D), lambda i, j: (0, 0)),
                  pl.BlockSpec((D, tn), lambda i, j: (0, j)),
                  pl.BlockSpec((D, LANES), lambda i, j: (0, 0))],
        out_specs=[pl.BlockSpec((tn // LANES, tm, LANES), lambda i, j: (j, i, 0)),
                   pl.BlockSpec((tm, LANES), lambda i, j: (i, 0))],
        out_shape=[jax.ShapeDtypeStruct((N // LANES, M, LANES), BF16),
                   jax.ShapeDtypeStruct((M, LANES), F32)],
        scratch_shapes=[pltpu.VMEM((tm, D), BF16)],
        compiler_params=_cparams(2),
        name="norm_proj",
    )(h, nw, wbig, wsmall)


def _softplus(x):
    return jnp.maximum(x, 0.0) + jnp.log1p(jnp.exp(-jnp.abs(x)))


def _prep_kernel(qkv_ref, prev_ref, sm_ref, cw_ref, prm_ref,
                 q_ref, qg_ref, k_ref, kw_ref, kd_ref, vb_ref,
                 db_ref, dc_ref, el_ref, cq_ref, carry_ref, *, tp):
    t = pl.program_id(1)
    nh = GDN_HEADS
    nck = tp // GDN_CHUNK

    @pl.when(t == 0)
    def _():
        carry_ref[...] = jnp.zeros_like(carry_ref)

    sm = sm_ref[...]
    a_log = prm_ref[0:1, :]
    dt_bias = prm_ref[1:2, :]
    f_bias = prm_ref[2:3, :]
    beta = jax.nn.sigmoid(sm)
    g = -jnp.exp(a_log) * _softplus(sm + dt_bias)
    logf = -_softplus(-(sm + f_bias))

    row = lax.broadcasted_iota(jnp.int32, (tp, tp), 0)
    col = lax.broadcasted_iota(jnp.int32, (tp, tp), 1)
    tri_full = (row >= col).astype(F32)
    csh = GDN_CHUNK.bit_length() - 1
    tri_chunk = jnp.where((row >> csh) == (col >> csh), tri_full, 0.0)
    hi = lax.Precision.HIGHEST
    gcum = jnp.dot(tri_chunk, g, precision=hi, preferred_element_type=F32)
    ccum = jnp.dot(tri_full, logf, precision=hi, preferred_element_type=F32) + carry_ref[0:1, :]
    carry_ref[0:1, :] = ccum[tp - 1:tp, :]

    glast = jnp.concatenate(
        [jnp.broadcast_to(gcum[(c + 1) * GDN_CHUNK - 1:(c + 1) * GDN_CHUNK, :], (GDN_CHUNK, LANES))
         for c in range(nck)], axis=0)
    exp_g = jnp.exp(gcum)
    exp_kd = jnp.exp(glast - gcum)
    exp_last = jnp.exp(glast)

    comb_t = gcum.T

    ntile = qkv_ref.shape[0]
    prev_ok = (t > 0).astype(F32)
    ci = lax.broadcasted_iota(jnp.int32, (GDN_CHUNK, GDN_CHUNK), 0)
    cj = lax.broadcasted_iota(jnp.int32, (GDN_CHUNK, GDN_CHUNK), 1)

    def conv_tile(idx):
        cur = qkv_ref[idx].astype(F32)
        tail = prev_ref[idx].astype(F32)[8:16, :] * prev_ok
        xc = jnp.concatenate([tail, cur], axis=0)
        w = cw_ref[:, idx, :]
        acc = xc[8:8 + tp, :] * w[3:4, :]
        for j in range(CONV_WIDTH - 1):
            sh = CONV_WIDTH - 1 - j
            acc = acc + xc[8 - sh:8 - sh + tp, :] * w[j:j + 1, :]
        return acc * jax.nn.sigmoid(acc)

    def l2n(x):
        return x * lax.rsqrt(jnp.sum(x * x, axis=-1, keepdims=True) + L2_EPS)

    scale = HEAD_DIM ** -0.5
    for h in range(nh):
        qh = l2n(conv_tile(h)) * scale
        kh = l2n(conv_tile(nh + h))
        vh = conv_tile(2 * nh + h)
        b_col = beta[:, SM_BETA + h:SM_BETA + h + 1]
        eg_col = exp_g[:, SM_A + h:SM_A + h + 1]
        ekd_col = exp_kd[:, SM_A + h:SM_A + h + 1]
        q_ref[h] = qh.astype(BF16)
        qg_ref[h] = (qh * eg_col).astype(BF16)
        k_ref[h] = kh.astype(BF16)
        kw_ref[h] = (kh * (b_col * eg_col)).astype(BF16)
        kd_ref[h] = (kh * ekd_col).astype(BF16)
        vb_ref[h] = (vh * b_col).astype(BF16)
        g_col = gcum[:, SM_A + h:SM_A + h + 1]
        db_tiles = []
        for c in range(nck):
            rs = slice(c * GDN_CHUNK, (c + 1) * GDN_CHUNK)
            diff = g_col[rs, :] - comb_t[SM_A + h:SM_A + h + 1, rs]
            dec = jnp.exp(jnp.where(ci >= cj, diff, -jnp.inf))
            dc_ref[0, h, c] = dec
            db_tiles.append(jnp.where(ci > cj, dec * b_col[rs, :], 0.0))
        for gp in range(nck // GDN_PACK):
            db_ref[0, h, gp] = jnp.concatenate(db_tiles[gp * GDN_PACK:(gp + 1) * GDN_PACK], axis=1)
        for c in range(nck):
            el_ref[0, h, c] = exp_last[c * GDN_CHUNK:c * GDN_CHUNK + 1, SM_A + h:SM_A + h + 1] * jnp.ones((1, LANES), F32)
        cq_ref[0, h] = jnp.broadcast_to(ccum[:, SM_F + h:SM_F + h + 1], (tp, LANES))


def gdn_prep(big, small, conv_w, prm, *, B, S, tp):
    M = B * S
    nt = S // tp
    nh = GDN_HEADS
    nqkv = 3 * nh
    nck = tp // GDN_CHUNK
    NC = S // GDN_CHUNK
    kern = functools.partial(_prep_kernel, tp=tp)
    head_spec = pl.BlockSpec((nh, tp, LANES), lambda b, t: (0, b * nt + t, 0))
    head_shape = jax.ShapeDtypeStruct((nh, M, LANES), BF16)
    return pl.pallas_call(
        kern,
        grid=(B, nt),
        in_specs=[pl.BlockSpec((nqkv, tp, LANES), lambda b, t: (0, b * nt + t, 0)),
                  pl.BlockSpec((nqkv, 16, LANES),
                               lambda b, t: (0, jnp.maximum((b * nt + t) * (tp // 16) - 1, 0), 0)),
                  pl.BlockSpec((tp, LANES), lambda b, t: (b * nt + t, 0)),
                  pl.BlockSpec((CONV_WIDTH, nqkv, LANES), lambda b, t: (0, 0, 0)),
                  pl.BlockSpec((8, LANES), lambda b, t: (0, 0))],
        out_specs=[head_spec] * 6 + [
            pl.BlockSpec((1, nh, nck // GDN_PACK, GDN_CHUNK, GDN_PACK * GDN_CHUNK), lambda b, t: (b, 0, t, 0, 0)),
            pl.BlockSpec((1, nh, nck, GDN_CHUNK, GDN_CHUNK), lambda b, t: (b, 0, t, 0, 0)),
            pl.BlockSpec((1, nh, nck, 1, LANES), lambda b, t: (b, 0, t, 0, 0)),
            pl.BlockSpec((1, nh, tp, LANES), lambda b, t: (b, 0, t, 0)),
        ],
        out_shape=[head_shape] * 6 + [
            jax.ShapeDtypeStruct((B, nh, NC // GDN_PACK, GDN_CHUNK, GDN_PACK * GDN_CHUNK), F32),
            jax.ShapeDtypeStruct((B, nh, NC, GDN_CHUNK, GDN_CHUNK), F32),
            jax.ShapeDtypeStruct((B, nh, NC, 1, LANES), F32),
            jax.ShapeDtypeStruct((B, nh, S, LANES), F32),
        ],
        scratch_shapes=[pltpu.VMEM((8, LANES), F32)],
        compiler_params=_cparams(2),
        name="gdn_prep",
    )(big, big, small, conv_w, prm)


def _bdot(a, b):
    return lax.dot_general(a, b, (((2,), (1,)), ((0,), (0,))), preferred_element_type=F32)


def _gdn_kernel(q_ref, qg_ref, k_ref, kw_ref, kd_ref, vb_ref, z_ref, db_ref, dc_ref, el_ref, nw_ref,
                o_ref, s_ref, *, tg, hb):
    C = GDN_CHUNK
    nck = tg // C

    @pl.when(pl.program_id(2) == 0)
    def _():
        s_ref[...] = jnp.zeros_like(s_ref)

    npk = GDN_PACK
    gw = npk * C
    ngrp = nck // npk
    ii = lax.broadcasted_iota(jnp.int32, (C, gw), 0)
    ll = lax.broadcasted_iota(jnp.int32, (C, gw), 1)
    jj = ll & (C - 1)
    blk = [(ll >> (C.bit_length() - 1)) == b for b in range(npk)]
    eyecat = (ii == jj).astype(F32)

    def bdiag(xcat):
        return jnp.concatenate([jnp.where(blk[b], xcat, 0.0) for b in range(npk)], axis=0).astype(BF16)

    attns = []
    for hh in range(hb):
        k3 = k_ref[hh].reshape(nck, C, HEAD_DIM)
        q3 = q_ref[hh].reshape(nck, C, HEAD_DIM)
        qk = lax.dot_general(q3, k3, (((2,), (2,)), ((0,), (0,))), preferred_element_type=F32)
        attns.append((qk * dc_ref[0, hh]).astype(BF16))

    solves = [(hh, g) for hh in range(hb) for g in range(ngrp)]
    mcats = []
    for hh, g in solves:
        kg = k_ref[hh, g * gw:(g + 1) * gw, :]
        gram = lax.dot_general(kg, kg, (((1,), (1,)), ((), ())), preferred_element_type=F32)
        mcat = jnp.where(blk[0], gram[:C, :], 0.0)
        for b in range(1, npk):
            mcat = mcat + jnp.where(blk[b], gram[b * C:(b + 1) * C, :], 0.0)
        mcats.append(mcat * db_ref[0, hh, g])

    xcats = [eyecat for _ in solves]
    s = 1
    while s < C:
        ls = s.bit_length() - 1
        sel = ((ii >> (ls + 1)) == (jj >> (ls + 1))) & (((ii >> ls) & 1) == 1) & (((jj >> ls) & 1) == 0)
        ys = [jnp.dot(x.astype(BF16), bdiag(jnp.where(sel, m, 0.0)), preferred_element_type=F32)
              for x, m in zip(xcats, mcats)]
        zs = [jnp.dot(y.astype(BF16), bdiag(x), preferred_element_type=F32) for y, x in zip(ys, xcats)]
        xcats = [x - z for x, z in zip(xcats, zs)]
        s *= 2

    us = [[None] * nck for _ in range(hb)]
    ws = [[None] * nck for _ in range(hb)]
    for (hh, g), xcat in zip(solves, xcats):
        rows = slice(g * gw, (g + 1) * gw)
        rhs = jnp.concatenate([vb_ref[hh, rows, :], kw_ref[hh, rows, :]], axis=1)
        for b in range(npk):
            sol = jnp.dot(jnp.where(blk[b], xcat, 0.0).astype(BF16), rhs, preferred_element_type=F32)
            us[hh][g * npk + b] = sol[:, :HEAD_DIM]
            ws[hh][g * npk + b] = sol[:, HEAD_DIM:].astype(BF16)

    nw = nw_ref[...]
    sts = [s_ref[hh] for hh in range(hb)]
    heads = range(hb)
    for c in range(nck):
        rs = slice(c * C, (c + 1) * C)
        sbs = [sts[hh].astype(BF16) for hh in heads]
        r1s = [jnp.dot(jnp.concatenate([ws[hh][c], qg_ref[hh, rs, :]], axis=0), sbs[hh],
                       preferred_element_type=F32) for hh in heads]
        vns = [(us[hh][c] - r1s[hh][:C, :]).astype(BF16) for hh in heads]
        outs = [r1s[hh][C:, :] + jnp.dot(attns[hh][c], vns[hh], preferred_element_type=F32)
                for hh in heads]
        sts = [sts[hh] * el_ref[0, hh, c] + lax.dot_general(
            kd_ref[hh, rs, :], vns[hh], (((0,), (0,)), ((), ())), preferred_element_type=F32)
            for hh in heads]
        for hh in heads:
            z = z_ref[hh, rs, :].astype(F32)
            o = _rms_rows(outs[hh], nw) * (z * jax.nn.sigmoid(z))
            o_ref[rs, hh * HEAD_DIM:(hh + 1) * HEAD_DIM] = o.astype(BF16)
    for hh in range(hb):
        s_ref[hh] = sts[hh]


def gdn_core(q, qg, k, kw, kd, vb, big, db, dc, el, gdn_nw, *, B, S, tg, hb):
    M = B * S
    nt = S // tg
    nh = GDN_HEADS
    nck = tg // GDN_CHUNK
    z_blk0 = 3 * nh // hb
    assert nh % hb == 0 and (3 * nh) % hb == 0
    hs = pl.BlockSpec((hb, tg, LANES), lambda b, h, t: (h, b * nt + t, 0))
    ds_ = pl.BlockSpec((1, hb, nck, GDN_CHUNK, GDN_CHUNK), lambda b, h, t: (b, h, t, 0, 0))
    dbs = pl.BlockSpec((1, hb, nck // GDN_PACK, GDN_CHUNK, GDN_PACK * GDN_CHUNK),
                       lambda b, h, t: (b, h, t, 0, 0))
    return pl.pallas_call(
        functools.partial(_gdn_kernel, tg=tg, hb=hb),
        grid=(B, nh // hb, nt),
        in_specs=[hs] * 6 + [
            pl.BlockSpec((hb, tg, LANES), lambda b, h, t: (z_blk0 + h, b * nt + t, 0)),
            dbs, ds_,
            pl.BlockSpec((1, hb, nck, 1, LANES), lambda b, h, t: (b, h, t, 0, 0)),
            pl.BlockSpec((1, LANES), lambda b, h, t: (0, 0))],
        out_specs=pl.BlockSpec((tg, hb * HEAD_DIM), lambda b, h, t: (b * nt + t, h)),
        out_shape=jax.ShapeDtypeStruct((M, nh * HEAD_DIM), BF16),
        scratch_shapes=[pltpu.VMEM((hb, HEAD_DIM, HEAD_DIM), F32)],
        compiler_params=_cparams(3),
        name="gdn_core",
    )(q, qg, k, kw, kd, vb, big, db, dc, el, gdn_nw)


LOG2E = 1.4426950408889634


def _split3(c):
    hi = c.astype(BF16).astype(F32)
    r1 = c - hi
    mid = r1.astype(BF16).astype(F32)
    lo = (r1 - mid).astype(BF16).astype(F32)
    return hi, mid, lo


def _fox_aug(c, sign_c, ones_first):
    lane = lax.broadcasted_iota(jnp.int32, c.shape, 1)
    hi, mid, lo = _split3(c)
    k3 = lane - 3 * (lane >= 3).astype(jnp.int32)
    cpart = jnp.where(k3 == 0, hi, jnp.where(k3 == 1, mid, lo)) * sign_c
    c_lanes = (lane >= 3) if ones_first else (lane < 3)
    a = jnp.where(lane < 6, jnp.where(c_lanes, cpart, 1.0), 0.0)
    return a.astype(BF16)


def _fox_kernel(q_ref, k_ref, v_ref, c_ref, o_ref, ka_ref, va_ref, qa_ref, m_ref, acc_ref, *, tq, tk, S):
    i = pl.program_id(2)
    rows = min(256, S)

    @pl.when(i == 0)
    def _():
        def body(r, carry):
            sl = pl.ds(pl.multiple_of(r * rows, rows), rows)
            ka_ref[sl, :LANES] = k_ref[0, sl, :]
            ka_ref[sl, LANES:] = _fox_aug(c_ref[0, 0, sl, :] * LOG2E, -1.0, True)
            va_ref[sl, :LANES] = v_ref[0, sl, :]
            va_ref[sl, LANES:] = jnp.ones((rows, LANES), BF16)
            return carry
        lax.fori_loop(0, S // rows, body, 0)

    q_rows = pl.ds(pl.multiple_of(i * tq, tq), tq)
    qa_ref[:, :LANES] = (q_ref[0].astype(F32) * (HEAD_DIM ** -0.5 * LOG2E)).astype(BF16)
    qa_ref[:, LANES:] = _fox_aug(c_ref[0, 0, q_rows, :] * LOG2E, 1.0, False)
    m_ref[...] = jnp.full(m_ref.shape, -jnp.inf, F32)
    acc_ref[...] = jnp.zeros(acc_ref.shape, F32)

    def block(j, r0, masked):
        nr = tq - r0
        ks = pl.ds(pl.multiple_of(j * tk, tk), tk)
        s = lax.dot_general(qa_ref[r0:, :], ka_ref[ks, :], (((1,), (1,)), ((), ())),
                            preferred_element_type=F32)
        if masked:
            r = lax.broadcasted_iota(jnp.int32, (nr, tk), 0) + (i * tq + r0)
            c = lax.broadcasted_iota(jnp.int32, (nr, tk), 1) + j * tk
            s = jnp.where(c <= r, s, -jnp.inf)
        m_prev = m_ref[r0:, :]
        m_new = jnp.maximum(m_prev, jnp.max(s, axis=-1, keepdims=True))
        m_ref[r0:, :] = m_new
        p = jnp.exp2(s - jnp.concatenate([m_new] * (tk // LANES), axis=1)).astype(BF16)
        alpha = jnp.exp2(m_prev - m_new)
        acc_ref[r0:, :] = (jnp.concatenate([alpha, alpha], axis=1) * acc_ref[r0:, :]
                           + jnp.dot(p, va_ref[ks, :], preferred_element_type=F32))

    npb = tq // tk

    def full_body(jb, carry):
        for d in range(npb):
            block(jb * npb + d, 0, False)
        return carry

    lax.fori_loop(0, i, full_body, 0)
    for d in range(npb):
        block(i * npb + d, d * tk, True)
    acc = acc_ref[...]
    o_ref[...] = (acc[:, :HEAD_DIM] / acc[:, HEAD_DIM:]).astype(BF16)


def fox_attention(big, c, *, B, S, tq, tk):
    M = B * S
    nq = S // tq
    nh = FOX_HEADS
    q0 = 4 * GDN_HEADS
    k0 = q0 + nh
    v0 = k0 + nh
    return pl.pallas_call(
        functools.partial(_fox_kernel, tq=tq, tk=tk, S=S),
        grid=(B, nh, nq),
        in_specs=[pl.BlockSpec((1, tq, LANES), lambda b, h, i: (q0 + h, b * nq + i, 0)),
                  pl.BlockSpec((1, S, LANES), lambda b, h, i: (k0 + h, b, 0)),
                  pl.BlockSpec((1, S, LANES), lambda b, h, i: (v0 + h, b, 0)),
                  pl.BlockSpec((1, 1, S, LANES), lambda b, h, i: (b, h, 0, 0))],
        out_specs=pl.BlockSpec((tq, LANES), lambda b, h, i: (b * nq + i, h)),
        out_shape=jax.ShapeDtypeStruct((M, nh * HEAD_DIM), BF16),
        scratch_shapes=[pltpu.VMEM((S, 2 * LANES), BF16),
                        pltpu.VMEM((S, 2 * LANES), BF16),
                        pltpu.VMEM((tq, 2 * LANES), BF16),
                        pltpu.VMEM((tq, LANES), F32),
                        pltpu.VMEM((tq, 2 * HEAD_DIM), F32)],
        compiler_params=_cparams(3),
        name="fox_attention",
    )(big, big, big, c)


def _out_proj_kernel(og_ref, of_ref, w1_ref, w2_ref, h_ref, o_ref):
    acc = jnp.dot(og_ref[...], w1_ref[...], preferred_element_type=F32)
    acc = acc + jnp.dot(of_ref[...], w2_ref[...], preferred_element_type=F32)
    o_ref[...] = h_ref[...] + acc


def out_proj(og, of, w_out, h, *, tm, tn):
    M, D = h.shape
    Kg = og.shape[1]
    return pl.pallas_call(
        _out_proj_kernel,
        grid=(M // tm, D // tn),
        in_specs=[pl.BlockSpec((tm, Kg), lambda i, j: (i, 0)),
                  pl.BlockSpec((tm, Kg), lambda i, j: (i, 0)),
                  pl.BlockSpec((Kg, tn), lambda i, j: (0, j)),
                  pl.BlockSpec((Kg, tn), lambda i, j: (1, j)),
                  pl.BlockSpec((tm, tn), lambda i, j: (i, j))],
        out_specs=pl.BlockSpec((tm, tn), lambda i, j: (i, j)),
        out_shape=jax.ShapeDtypeStruct((M, D), F32),
        compiler_params=_cparams(2),
        name="out_proj",
    )(og, of, w_out, w_out, h)


def _mlp_kernel(h_ref, nw_ref, wu_ref, wd_ref, fw_ref, o_ref, hn_ref, *, final_norm):
    f = pl.program_id(1)

    @pl.when(f == 0)
    def _():
        _norm_to(h_ref, nw_ref, hn_ref, 32)
        o_ref[...] = h_ref[...]

    a = jnp.dot(hn_ref[...], wu_ref[...], preferred_element_type=F32)
    a = jnp.maximum(a, 0.0)
    a = (a * a).astype(BF16)
    o_ref[...] += jnp.dot(a, wd_ref[...], preferred_element_type=F32)

    if final_norm:
        @pl.when(f == pl.num_programs(1) - 1)
        def _():
            _norm_to(o_ref, fw_ref, o_ref, 32)


def mlp(h, nw, w_up, w_down, fw, *, tm, tf, final_norm):
    M, D = h.shape
    F = w_up.shape[1]
    return pl.pallas_call(
        functools.partial(_mlp_kernel, final_norm=final_norm),
        grid=(M // tm, F // tf),
        in_specs=[pl.BlockSpec((tm, D), lambda i, f: (i, 0)),
                  pl.BlockSpec((1, D), lambda i, f: (0, 0)),
                  pl.BlockSpec((D, tf), lambda i, f: (0, f)),
                  pl.BlockSpec((tf, D), lambda i, f: (f, 0)),
                  pl.BlockSpec((1, D), lambda i, f: (0, 0))],
        out_specs=pl.BlockSpec((tm, D), lambda i, f: (i, 0)),
        out_shape=jax.ShapeDtypeStruct((M, D), F32),
        scratch_shapes=[pltpu.VMEM((tm, D), BF16)],
        compiler_params=_cparams(2),
        name="mlp",
    )(h, nw, w_up, w_down, fw)


def _rmsnorm_kernel(h_ref, nw_ref, o_ref):
    _norm_to(h_ref, nw_ref, o_ref, 32)


def rmsnorm(h, nw, *, tm):
    M, D = h.shape
    return pl.pallas_call(
        _rmsnorm_kernel,
        grid=(M // tm,),
        in_specs=[pl.BlockSpec((tm, D), lambda i: (i, 0)),
                  pl.BlockSpec((1, D), lambda i: (0, 0))],
        out_specs=pl.BlockSpec((tm, D), lambda i: (i, 0)),
        out_shape=jax.ShapeDtypeStruct((M, D), F32),
        compiler_params=_cparams(1),
        name="rmsnorm",
    )(h, nw)


def _s5_assemble(bd_ref, pe_ref, po_ref, q_ref, wt_ref, wi_ref, wo_ref):
    L = S5_L
    P = S5_STATE
    half = LANES // 2
    zero_tile = jnp.zeros((LANES, LANES), BF16)
    for s in range(L):
        for t in range(L):
            wt_ref[s * LANES:(s + 1) * LANES, t * LANES:(t + 1) * LANES] = (
                bd_ref[0, t - s] if t >= s else zero_tile)
    rgrp = lax.broadcasted_iota(jnp.int32, (LANES, LANES), 0) >> 4
    lane = lax.broadcasted_iota(jnp.int32, (LANES, LANES), 1)
    for s in range(L):
        pe = pe_ref[0, s]
        po = po_ref[0, s]
        for r in range(2):
            lo_src, hi_src = (pe, po) if r == 0 else (po, pe)
            for qd in range(S5_GPT // 2):
                tile = jnp.where((rgrp == 2 * qd) & (lane < half), lo_src,
                                 jnp.where((rgrp == 2 * qd + 1) & (lane >= half), hi_src, zero_tile))
                c0 = (r * (S5_GPT // 2) + qd) * LANES
                wi_ref[s * LANES:(s + 1) * LANES, c0:c0 + LANES] = tile
    lgrp = lax.broadcasted_iota(jnp.int32, (P, LANES), 1) >> 4
    zero_rows = jnp.zeros((P, LANES), BF16)
    for r in range(2):
        for t in range(L):
            src = q_ref[0, r, t]
            for h in range(S5_GPT):
                r0 = (r * S5_GPT + h) * P
                wo_ref[r0:r0 + P, t * LANES:(t + 1) * LANES] = jnp.where(lgrp == h, src, zero_rows)


def _s5_kernel(u_ref, bd_ref, pe_ref, po_ref, q_ref, ap_ref, d_ref, y_ref,
               wt_ref, wi_ref, wo_ref, x_ref, yn_ref, carry_ref, *, ts):
    L = S5_L
    ncb = ts // L
    nst = S5_GPT * S5_STATE

    @pl.when((pl.program_id(1) == 0) & (pl.program_id(2) == 0))
    def _():
        _s5_assemble(bd_ref, pe_ref, po_ref, q_ref, wt_ref, wi_ref, wo_ref)

    @pl.when(pl.program_id(2) == 0)
    def _():
        carry_ref[...] = jnp.zeros_like(carry_ref)

    for s in range(L):
        x_ref[:, s * LANES:(s + 1) * LANES] = u_ref[pl.ds(s, ncb, stride=L), :].astype(BF16)
    xb = x_ref[...]
    y_intra = jnp.dot(xb, wt_ref[...], preferred_element_type=F32)
    v = jnp.dot(xb, wi_ref[...], preferred_element_type=F32)

    rows = lax.broadcasted_iota(jnp.int32, (ncb, nst), 0)
    vr, vi = v[:, :nst], v[:, nst:]
    er = jnp.where(rows == 0, carry_ref[0:1, :], pltpu.roll(vr, 1, axis=0))
    ei = jnp.where(rows == 0, carry_ref[1:2, :], pltpu.roll(vi, 1, axis=0))
    d = 1
    k = 1
    while d < ncb:
        ar = ap_ref[0, k:k + 1, :nst]
        ai = ap_ref[0, k:k + 1, nst:]
        sr = jnp.where(rows >= d, pltpu.roll(er, d, axis=0), 0.0)
        si = jnp.where(rows >= d, pltpu.roll(ei, d, axis=0), 0.0)
        er, ei = er + ar * sr - ai * si, ei + ar * si + ai * sr
        d *= 2
        k += 1
    a1r = ap_ref[0, 0:1, :nst]
    a1i = ap_ref[0, 0:1, nst:]
    lr, li = er[ncb - 1:ncb, :], ei[ncb - 1:ncb, :]
    carry_ref[0:1, :] = a1r * lr - a1i * li + vr[ncb - 1:ncb, :]
    carry_ref[1:2, :] = a1r * li + a1i * lr + vi[ncb - 1:ncb, :]

    hprev = jnp.concatenate([er, ei], axis=1).astype(BF16)
    y = y_intra + jnp.dot(hprev, wo_ref[...], preferred_element_type=F32)
    dsk = d_ref[0]
    for s in range(L):
        rs = pl.ds(s, ncb, stride=L)
        yn_ref[rs, :] = y[:, s * LANES:(s + 1) * LANES] + dsk * u_ref[rs, :]
    yn = yn_ref[...]
    c0 = math.sqrt(2.0 / math.pi)
    y_ref[...] = (0.5 * yn * (1.0 + jnp.tanh(c0 * (yn + 0.044715 * (yn * yn * yn))))).astype(BF16)


def s5_core(u, bd, pe, po, qo, apow, dskip, *, B, S, ts):
    M, D = u.shape
    nj = D // LANES
    ns = S // ts
    ncb = ts // S5_L
    nst = S5_GPT * S5_STATE
    lw = S5_L * LANES
    return pl.pallas_call(
        functools.partial(_s5_kernel, ts=ts),
        grid=(nj, B, ns),
        in_specs=[pl.BlockSpec((ts, LANES), lambda j, b, s: (b * ns + s, j)),
                  pl.BlockSpec((1, S5_L, LANES, LANES), lambda j, b, s: (j, 0, 0, 0)),
                  pl.BlockSpec((1, S5_L, LANES, LANES), lambda j, b, s: (j, 0, 0, 0)),
                  pl.BlockSpec((1, S5_L, LANES, LANES), lambda j, b, s: (j, 0, 0, 0)),
                  pl.BlockSpec((1, 2, S5_L, S5_STATE, LANES), lambda j, b, s: (j, 0, 0, 0, 0)),
                  pl.BlockSpec((1, apow.shape[1], 2 * nst), lambda j, b, s: (j, 0, 0)),
                  pl.BlockSpec((1, 1, LANES), lambda j, b, s: (j, 0, 0))],
        out_specs=pl.BlockSpec((ts, LANES), lambda j, b, s: (b * ns + s, j)),
        out_shape=jax.ShapeDtypeStruct((M, D), BF16),
        scratch_shapes=[pltpu.VMEM((lw, lw), BF16),
                        pltpu.VMEM((lw, 2 * nst), BF16),
                        pltpu.VMEM((2 * nst, lw), BF16),
                        pltpu.VMEM((ncb, lw), BF16),
                        pltpu.VMEM((ts, LANES), F32),
                        pltpu.VMEM((8, nst), F32)],
        compiler_params=_cparams(3),
        name="s5_core",
    )(u, bd, pe, po, qo, apow, dskip)


def s5_operators(lam_re, lam_im, b_re, b_im, c_re, c_im, d_skip, log_dt, ncb):
    G, P = lam_re.shape
    L = S5_L
    gc = S5_GROUP
    lam = lax.complex(jnp.minimum(lam_re.astype(F32), -1e-4), lam_im.astype(F32))
    dt = jnp.exp(log_dt.astype(F32))[:, None]
    lam_dt = lam * dt
    lam_bar = jnp.exp(lam_dt)
    b_bar = ((lam_bar - 1.0) / lam)[:, :, None] * lax.complex(b_re.astype(F32), b_im.astype(F32))
    c = lax.complex(c_re.astype(F32), c_im.astype(F32))
    kk = jnp.arange(L + 1, dtype=F32)
    pows = jnp.exp(lam_dt[None] * kk[:, None, None])

    nj = G // S5_GPT
    kern = jnp.einsum("gcp,kgp,gpd->kgcd", c, pows[:L], b_bar).real
    eye = jnp.eye(S5_GPT, dtype=F32)
    bd = jnp.einsum("kjgcd,gh->jkgdhc", kern.reshape(L, nj, S5_GPT, gc, gc), eye)
    bd = bd.reshape(nj, L, LANES, LANES).astype(BF16)

    sidx = jnp.arange(L)
    min_c = pows[L - 1 - sidx][:, :, :, None] * b_bar[None]
    min_c = min_c.transpose(1, 0, 3, 2).reshape(nj, S5_GPT, L, gc, P).transpose(0, 2, 1, 3, 4)
    pe = jnp.concatenate([min_c.real, min_c.imag], axis=-1).reshape(nj, L, LANES, 2 * P).astype(BF16)
    po = jnp.concatenate([min_c.imag, min_c.real], axis=-1).reshape(nj, L, LANES, 2 * P).astype(BF16)

    mout_c = c[None] * pows[1:L + 1][:, :, None, :]
    mout_c = mout_c.reshape(L, nj, S5_GPT, gc, P).transpose(1, 0, 4, 2, 3)
    qo = jnp.stack([mout_c.real, -mout_c.imag], axis=1).reshape(nj, 2, L, P, LANES).astype(BF16)

    nsteps = max(1, int(math.log2(ncb)))
    mult = jnp.asarray([1.0] + [2.0 ** k for k in range(nsteps)], F32) * L
    ap = jnp.exp(lam_dt[None] * mult[:, None, None])
    ap = jnp.stack([ap.real, ap.imag], axis=1).reshape(1 + nsteps, 2, nj, S5_GPT * P)
    ap = ap.transpose(2, 0, 1, 3).reshape(nj, 1 + nsteps, 2 * S5_GPT * P)
    pad = (-ap.shape[1]) % 8
    ap = jnp.pad(ap, ((0, 0), (0, pad), (0, 0)))
    dsk = d_skip.astype(F32).reshape(nj, 1, LANES)
    return bd, pe, po, qo, ap, dsk


def _glu_kernel(y_ref, wv_ref, wg_ref, h_ref, o_ref):
    y = y_ref[...]
    val = jnp.dot(y, wv_ref[...], preferred_element_type=F32)
    gate = jnp.dot(y, wg_ref[...], preferred_element_type=F32)
    o_ref[...] = h_ref[...] + val * jax.nn.sigmoid(gate)


def glu_proj(y, w_glu, h, *, tm, tn):
    M, D = h.shape
    K = y.shape[1]
    ng = D // tn
    return pl.pallas_call(
        _glu_kernel,
        grid=(M // tm, ng),
        in_specs=[pl.BlockSpec((tm, K), lambda i, j: (i, 0)),
                  pl.BlockSpec((K, tn), lambda i, j: (0, j)),
                  pl.BlockSpec((K, tn), lambda i, j: (0, ng + j)),
                  pl.BlockSpec((tm, tn), lambda i, j: (i, j))],
        out_specs=pl.BlockSpec((tm, tn), lambda i, j: (i, j)),
        out_shape=jax.ShapeDtypeStruct((M, D), F32),
        compiler_params=_cparams(2),
        name="glu_proj",
    )(y, w_glu, w_glu, h)


def _tile(n, pref):
    t = min(n, pref)
    assert n % t == 0, (n, pref)
    return t


def _pack_in_proj(w_in):
    gw = GDN_HEADS * HEAD_DIM
    fw = FOX_HEADS * HEAD_DIM
    o = 4 * gw
    big = jnp.concatenate([w_in[:, :o], w_in[:, o + 2 * GDN_HEADS:o + 2 * GDN_HEADS + 3 * fw]], axis=1)
    small = jnp.concatenate([w_in[:, o:o + 2 * GDN_HEADS], w_in[:, o + 2 * GDN_HEADS + 3 * fw:]], axis=1)
    small = jnp.pad(small, ((0, 0), (0, LANES - small.shape[1])))
    return big.astype(BF16), small.astype(BF16)


def _gate_params(a_log, dt_bias, f_bias):
    prm = jnp.zeros((8, LANES), F32)
    prm = prm.at[0, SM_A:SM_A + GDN_HEADS].set(a_log.astype(F32))
    prm = prm.at[1, SM_A:SM_A + GDN_HEADS].set(dt_bias.astype(F32))
    prm = prm.at[2, SM_F:SM_F + FOX_HEADS].set(f_bias.astype(F32))
    return prm


def kernel(x, norm_mix, norm_mlp, norm_final, w_in, conv_qkv, gdn_a_log, gdn_dt_bias, gdn_norm, fox_f_bias, w_out, s5_lambda_re, s5_lambda_im, s5_b_re, s5_b_im, s5_c_re, s5_c_im, s5_d, s5_log_dt, w_glu, w_up, w_down):
    B, S, D = x.shape
    M = B * S
    depth = norm_mix.shape[0]
    h = x.reshape(M, D).astype(F32)

    tm = _tile(M, 512)
    tm_wide = _tile(M, 1024)
    tn = _tile(D, 512)
    tf = _tile(w_up.shape[2], 512)
    tp = _tile(S, 256)
    tg = _tile(S, 512)
    tq = _tile(S, 1024)
    tk = _tile(tq, 256)
    ts = _tile(S, 4096)
    ncb = ts // S5_L

    for layer in range(depth):
        i = layer // 2
        nw = norm_mix[layer].reshape(1, D).astype(F32)
        if layer % 2 == 0:
            wbig, wsmall = _pack_in_proj(w_in[i])
            big, small = norm_proj(h, nw, wbig, wsmall, tm=tm, tn=_tile(wbig.shape[1], 1024))
            cw = conv_qkv[i].astype(F32).reshape(CONV_WIDTH, 3 * GDN_HEADS, LANES)
            prm = _gate_params(gdn_a_log[i], gdn_dt_bias[i], fox_f_bias[i])
            q, qg, k, kw, kd, vb, db, dc, el, cf = gdn_prep(big, small, cw, prm, B=B, S=S, tp=tp)
            o_gdn = gdn_core(q, qg, k, kw, kd, vb, big, db, dc, el,
                             gdn_norm[i].reshape(1, HEAD_DIM).astype(F32), B=B, S=S, tg=tg, hb=GDN_HB)
            o_fox = fox_attention(big, cf, B=B, S=S, tq=tq, tk=tk)
            h = out_proj(o_gdn, o_fox, w_out[i].astype(BF16), h, tm=tm_wide, tn=tn)
        else:
            u = rmsnorm(h, nw, tm=tm)
            ops = s5_operators(s5_lambda_re[i], s5_lambda_im[i], s5_b_re[i], s5_b_im[i],
                               s5_c_re[i], s5_c_im[i], s5_d[i], s5_log_dt[i], ncb)
            y = s5_core(u, *ops, B=B, S=S, ts=ts)
            h = glu_proj(y, w_glu[i].astype(BF16), h, tm=tm_wide, tn=tn)
        last = layer == depth - 1
        h = mlp(h, norm_mlp[layer].reshape(1, D).astype(F32), w_up[layer].astype(BF16),
                w_down[layer].astype(BF16), norm_final.reshape(1, D).astype(F32),
                tm=tm, tf=tf, final_norm=last)
    return h.reshape(B, S, D).astype(x.dtype)
```

```python
import functools
import math

import jax
import jax.numpy as jnp
from jax import lax
from jax.experimental import pallas as pl
from jax.experimental.pallas import tpu as pltpu

F32 = jnp.float32
BF16 = jnp.bfloat16

LANES = 128
HEAD_DIM = 128
GDN_HEADS = 16
FOX_HEADS = 16
CONV_WIDTH = 4
GDN_CHUNK = 64
GDN_HB = 8
GDN_PACK = 4
S5_GROUP = 16
S5_STATE = 64
S5_L = 16
S5_GPT = LANES // S5_GROUP
RMS_EPS = 1e-6
L2_EPS = 1e-6
VMEM_LIMIT = 60 * 1024 * 1024

SM_BETA, SM_A, SM_F = 0, 16, 32


def _cparams(n_axes):
    return pltpu.CompilerParams(dimension_semantics=("arbitrary",) * n_axes,
                                vmem_limit_bytes=VMEM_LIMIT)


def _rms_rows(x, w):
    ms = jnp.mean(x * x, axis=-1, keepdims=True)
    return x * lax.rsqrt(ms + RMS_EPS) * w


def _norm_to(h_ref, nw_ref, dst_ref, rows):
    tm = h_ref.shape[0]

    def body(r, c):
        sl = pl.ds(pl.multiple_of(r * rows, rows), rows)
        dst_ref[sl, :] = _rms_rows(h_ref[sl, :], nw_ref[...]).astype(dst_ref.dtype)
        return c

    lax.fori_loop(0, tm // rows, body, 0)


def _norm_proj_kernel(h_ref, nw_ref, wb_ref, ws_ref, big_ref, small_ref, hn_ref, *, tn):
    @pl.when(pl.program_id(1) == 0)
    def _():
        _norm_to(h_ref, nw_ref, hn_ref, 32)
        small_ref[...] = jnp.dot(hn_ref[...], ws_ref[...], preferred_element_type=F32)

    acc = jnp.dot(hn_ref[...], wb_ref[...], preferred_element_type=F32)
    for c in range(tn // LANES):
        big_ref[c] = acc[:, c * LANES:(c + 1) * LANES].astype(BF16)


def norm_proj(h, nw, wbig, wsmall, *, tm, tn):
    M, D = h.shape
    N = wbig.shape[1]
    return pl.pallas_call(
        functools.partial(_norm_proj_kernel, tn=tn),
        grid=(M // tm, N // tn),
        in_specs=[pl.BlockSpec((tm, D), lambda i, j: (i, 0)),
                  pl.BlockSpec((1, D), lambda i, j: (0, 0)),
                  pl.BlockSpec((D, tn), lambda i, j: (0, j)),
                  pl.BlockSpec((D, LANES), lambda i, j: (0, 0))],
        out_specs=[pl.BlockSpec((tn // LANES, tm, LANES), lambda i, j: (j, i, 0)),
                   pl.BlockSpec((tm, LANES), lambda i, j: (i, 0))],
        out_shape=[jax.ShapeDtypeStruct((N // LANES, M, LANES), BF16),
                   jax.ShapeDtypeStruct((M, LANES), F32)],
        scratch_shapes=[pltpu.VMEM((tm, D), BF16)],
        compiler_params=_cparams(2),
        name="norm_proj",
    )(h, nw, wbig, wsmall)


def _softplus(x):
    return jnp.maximum(x, 0.0) + jnp.log1p(jnp.exp(-jnp.abs(x)))


def _prep_kernel(qkv_ref, prev_ref, sm_ref, cw_ref, prm_ref,
                 q_ref, qg_ref, k_ref, kw_ref, kd_ref, vb_ref,
                 db_ref, dc_ref, el_ref, cq_ref, carry_ref, *, tp):
    t = pl.program_id(1)
    nh = GDN_HEADS
    nck = tp // GDN_CHUNK

    @pl.when(t == 0)
    def _():
        carry_ref[...] = jnp.zeros_like(carry_ref)

    sm = sm_ref[...]
    a_log = prm_ref[0:1, :]
    dt_bias = prm_ref[1:2, :]
    f_bias = prm_ref[2:3, :]
    beta = jax.nn.sigmoid(sm)
    g = -jnp.exp(a_log) * _softplus(sm + dt_bias)
    logf = -_softplus(-(sm + f_bias))

    row = lax.broadcasted_iota(jnp.int32, (tp, tp), 0)
    col = lax.broadcasted_iota(jnp.int32, (tp, tp), 1)
    tri_full = (row >= col).astype(F32)
    csh = GDN_CHUNK.bit_length() - 1
    tri_chunk = jnp.where((row >> csh) == (col >> csh), tri_full, 0.0)
    hi = lax.Precision.HIGHEST
    gcum = jnp.dot(tri_chunk, g, precision=hi, preferred_element_type=F32)
    ccum = jnp.dot(tri_full, logf, precision=hi, preferred_element_type=F32) + carry_ref[0:1, :]
    carry_ref[0:1, :] = ccum[tp - 1:tp, :]

    glast = jnp.concatenate(
        [jnp.broadcast_to(gcum[(c + 1) * GDN_CHUNK - 1:(c + 1) * GDN_CHUNK, :], (GDN_CHUNK, LANES))
         for c in range(nck)], axis=0)
    exp_g = jnp.exp(gcum)
    exp_kd = jnp.exp(glast - gcum)
    exp_last = jnp.exp(glast)

    comb_t = gcum.T

    ntile = qkv_ref.shape[0]
    prev_ok = (t > 0).astype(F32)
    ci = lax.broadcasted_iota(jnp.int32, (GDN_CHUNK, GDN_CHUNK), 0)
    cj = lax.broadcasted_iota(jnp.int32, (GDN_CHUNK, GDN_CHUNK), 1)

    def conv_tile(idx):
        cur = qkv_ref[idx].astype(F32)
        tail = prev_ref[idx].astype(F32)[8:16, :] * prev_ok
        xc = jnp.concatenate([tail, cur], axis=0)
        w = cw_ref[:, idx, :]
        acc = xc[8:8 + tp, :] * w[3:4, :]
        for j in range(CONV_WIDTH - 1):
            sh = CONV_WIDTH - 1 - j
            acc = acc + xc[8 - sh:8 - sh + tp, :] * w[j:j + 1, :]
        return acc * jax.nn.sigmoid(acc)

    def l2n(x):
        return x * lax.rsqrt(jnp.sum(x * x, axis=-1, keepdims=True) + L2_EPS)

    scale = HEAD_DIM ** -0.5
    for h in range(nh):
        qh = l2n(conv_tile(h)) * scale
        kh = l2n(conv_tile(nh + h))
        vh = conv_tile(2 * nh + h)
        b_col = beta[:, SM_BETA + h:SM_BETA + h + 1]
        eg_col = exp_g[:, SM_A + h:SM_A + h + 1]
        ekd_col = exp_kd[:, SM_A + h:SM_A + h + 1]
        q_ref[h] = qh.astype(BF16)
        qg_ref[h] = (qh * eg_col).astype(BF16)
        k_ref[h] = kh.astype(BF16)
        kw_ref[h] = (kh * (b_col * eg_col)).astype(BF16)
        kd_ref[h] = (kh * ekd_col).astype(BF16)
        vb_ref[h] = (vh * b_col).astype(BF16)
        g_col = gcum[:, SM_A + h:SM_A + h + 1]
        db_tiles = []
        for c in range(nck):
            rs = slice(c * GDN_CHUNK, (c + 1) * GDN_CHUNK)
            diff = g_col[rs, :] - comb_t[SM_A + h:SM_A + h + 1, rs]
            dec = jnp.exp(jnp.where(ci >= cj, diff, -jnp.inf))
            dc_ref[0, h, c] = dec
            db_tiles.append(jnp.where(ci > cj, dec * b_col[rs, :], 0.0))
        for gp in range(nck // GDN_PACK):
            db_ref[0, h, gp] = jnp.concatenate(db_tiles[gp * GDN_PACK:(gp + 1) * GDN_PACK], axis=1)
        for c in range(nck):
            el_ref[0, h, c] = exp_last[c * GDN_CHUNK:c * GDN_CHUNK + 1, SM_A + h:SM_A + h + 1] * jnp.ones((1, LANES), F32)
        cq_ref[0, h] = jnp.broadcast_to(ccum[:, SM_F + h:SM_F + h + 1], (tp, LANES))


def gdn_prep(big, small, conv_w, prm, *, B, S, tp):
    M = B * S
    nt = S // tp
    nh = GDN_HEADS
    nqkv = 3 * nh
    nck = tp // GDN_CHUNK
    NC = S // GDN_CHUNK
    kern = functools.partial(_prep_kernel, tp=tp)
    head_spec = pl.BlockSpec((nh, tp, LANES), lambda b, t: (0, b * nt + t, 0))
    head_shape = jax.ShapeDtypeStruct((nh, M, LANES), BF16)
    return pl.pallas_call(
        kern,
        grid=(B, nt),
        in_specs=[pl.BlockSpec((nqkv, tp, LANES), lambda b, t: (0, b * nt + t, 0)),
                  pl.BlockSpec((nqkv, 16, LANES),
                               lambda b, t: (0, jnp.maximum((b * nt + t) * (tp // 16) - 1, 0), 0)),
                  pl.BlockSpec((tp, LANES), lambda b, t: (b * nt + t, 0)),
                  pl.BlockSpec((CONV_WIDTH, nqkv, LANES), lambda b, t: (0, 0, 0)),
                  pl.BlockSpec((8, LANES), lambda b, t: (0, 0))],
        out_specs=[head_spec] * 6 + [
            pl.BlockSpec((1, nh, nck // GDN_PACK, GDN_CHUNK, GDN_PACK * GDN_CHUNK), lambda b, t: (b, 0, t, 0, 0)),
            pl.BlockSpec((1, nh, nck, GDN_CHUNK, GDN_CHUNK), lambda b, t: (b, 0, t, 0, 0)),
            pl.BlockSpec((1, nh, nck, 1, LANES), lambda b, t: (b, 0, t, 0, 0)),
            pl.BlockSpec((1, nh, tp, LANES), lambda b, t: (b, 0, t, 0)),
        ],
        out_shape=[head_shape] * 6 + [
            jax.ShapeDtypeStruct((B, nh, NC // GDN_PACK, GDN_CHUNK, GDN_PACK * GDN_CHUNK), F32),
            jax.ShapeDtypeStruct((B, nh, NC, GDN_CHUNK, GDN_CHUNK), F32),
            jax.ShapeDtypeStruct((B, nh, NC, 1, LANES), F32),
            jax.ShapeDtypeStruct((B, nh, S, LANES), F32),
        ],
        scratch_shapes=[pltpu.VMEM((8, LANES), F32)],
        compiler_params=_cparams(2),
        name="gdn_prep",
    )(big, big, small, conv_w, prm)


def _bdot(a, b):
    return lax.dot_general(a, b, (((2,), (1,)), ((0,), (0,))), preferred_element_type=F32)


def _gdn_kernel(q_ref, qg_ref, k_ref, kw_ref, kd_ref, vb_ref, z_ref, db_ref, dc_ref, el_ref, nw_ref,
                o_ref, s_ref, *, tg, hb):
    C = GDN_CHUNK
    nck = tg // C

    @pl.when(pl.program_id(2) == 0)
    def _():
        s_ref[...] = jnp.zeros_like(s_ref)

    npk = GDN_PACK
    gw = npk * C
    ngrp = nck // npk
    ii = lax.broadcasted_iota(jnp.int32, (C, gw), 0)
    ll = lax.broadcasted_iota(jnp.int32, (C, gw), 1)
    jj = ll & (C - 1)
    blk = [(ll >> (C.bit_length() - 1)) == b for b in range(npk)]
    eyecat = (ii == jj).astype(F32)

    def bdiag(xcat):
        return jnp.concatenate([jnp.where(blk[b], xcat, 0.0) for b in range(npk)], axis=0).astype(BF16)

    attns = []
    for hh in range(hb):
        k3 = k_ref[hh].reshape(nck, C, HEAD_DIM)
        q3 = q_ref[hh].reshape(nck, C, HEAD_DIM)
        qk = lax.dot_general(q3, k3, (((2,), (2,)), ((0,), (0,))), preferred_element_type=F32)
        attns.append((qk * dc_ref[0, hh]).astype(BF16))

    solves = [(hh, g) for hh in range(hb) for g in range(ngrp)]
    mcats = []
    for hh, g in solves:
        kg = k_ref[hh, g * gw:(g + 1) * gw, :]
        gram = lax.dot_general(kg, kg, (((1,), (1,)), ((), ())), preferred_element_type=F32)
        mcat = jnp.where(blk[0], gram[:C, :], 0.0)
        for b in range(1, npk):
            mcat = mcat + jnp.where(blk[b], gram[b * C:(b + 1) * C, :], 0.0)
        mcats.append(mcat * db_ref[0, hh, g])

    xcats = [eyecat for _ in solves]
    s = 1
    while s < C:
        ls = s.bit_length() - 1
        sel = ((ii >> (ls + 1)) == (jj >> (ls + 1))) & (((ii >> ls) & 1) == 1) & (((jj >> ls) & 1) == 0)
        ys = [jnp.dot(x.astype(BF16), bdiag(jnp.where(sel, m, 0.0)), preferred_element_type=F32)
              for x, m in zip(xcats, mcats)]
        zs = [jnp.dot(y.astype(BF16), bdiag(x), preferred_element_type=F32) for y, x in zip(ys, xcats)]
        xcats = [x - z for x, z in zip(xcats, zs)]
        s *= 2

    us = [[None] * nck for _ in range(hb)]
    ws = [[None] * nck for _ in range(hb)]
    for (hh, g), xcat in zip(solves, xcats):
        rows = slice(g * gw, (g + 1) * gw)
        rhs = jnp.concatenate([vb_ref[hh, rows, :], kw_ref[hh, rows, :]], axis=1)
        for b in range(npk):
            sol = jnp.dot(jnp.where(blk[b], xcat, 0.0).astype(BF16), rhs, preferred_element_type=F32)
            us[hh][g * npk + b] = sol[:, :HEAD_DIM]
            ws[hh][g * npk + b] = sol[:, HEAD_DIM:].astype(BF16)

    nw = nw_ref[...]
    sts = [s_ref[hh] for hh in range(hb)]
    heads = range(hb)
    for c in range(nck):
        rs = slice(c * C, (c + 1) * C)
        sbs = [sts[hh].astype(BF16) for hh in heads]
        r1s = [jnp.dot(jnp.concatenate([ws[hh][c], qg_ref[hh, rs, :]], axis=0), sbs[hh],
                       preferred_element_type=F32) for hh in heads]
        vns = [(us[hh][c] - r1s[hh][:C, :]).astype(BF16) for hh in heads]
        outs = [r1s[hh][C:, :] + jnp.dot(attns[hh][c], vns[hh], preferred_element_type=F32)
                for hh in heads]
        sts = [sts[hh] * el_ref[0, hh, c] + lax.dot_general(
            kd_ref[hh, rs, :], vns[hh], (((0,), (0,)), ((), ())), preferred_element_type=F32)
            for hh in heads]
        for hh in heads:
            z = z_ref[hh, rs, :].astype(F32)
            o = _rms_rows(outs[hh], nw) * (z * jax.nn.sigmoid(z))
            o_ref[rs, hh * HEAD_DIM:(hh + 1) * HEAD_DIM] = o.astype(BF16)
    for hh in range(hb):
        s_ref[hh] = sts[hh]


def gdn_core(q, qg, k, kw, kd, vb, big, db, dc, el, gdn_nw, *, B, S, tg, hb):
    M = B * S
    nt = S // tg
    nh = GDN_HEADS
    nck = tg // GDN_CHUNK
    z_blk0 = 3 * nh // hb
    assert nh % hb == 0 and (3 * nh) % hb == 0
    hs = pl.BlockSpec((hb, tg, LANES), lambda b, h, t: (h, b * nt + t, 0))
    ds_ = pl.BlockSpec((1, hb, nck, GDN_CHUNK, GDN_CHUNK), lambda b, h, t: (b, h, t, 0, 0))
    dbs = pl.BlockSpec((1, hb, nck // GDN_PACK, GDN_CHUNK, GDN_PACK * GDN_CHUNK),
                       lambda b, h, t: (b, h, t, 0, 0))
    return pl.pallas_call(
        functools.partial(_gdn_kernel, tg=tg, hb=hb),
        grid=(B, nh // hb, nt),
        in_specs=[hs] * 6 + [
            pl.BlockSpec((hb, tg, LANES), lambda b, h, t: (z_blk0 + h, b * nt + t, 0)),
            dbs, ds_,
            pl.BlockSpec((1, hb, nck, 1, LANES), lambda b, h, t: (b, h, t, 0, 0)),
            pl.BlockSpec((1, LANES), lambda b, h, t: (0, 0))],
        out_specs=pl.BlockSpec((tg, hb * HEAD_DIM), lambda b, h, t: (b * nt + t, h)),
        out_shape=jax.ShapeDtypeStruct((M, nh * HEAD_DIM), BF16),
        scratch_shapes=[pltpu.VMEM((hb, HEAD_DIM, HEAD_DIM), F32)],
        compiler_params=_cparams(3),
        name="gdn_core",
    )(q, qg, k, kw, kd, vb, big, db, dc, el, gdn_nw)


LOG2E = 1.4426950408889634


def _split3(c):
    hi = c.astype(BF16).astype(F32)
    r1 = c - hi
    mid = r1.astype(BF16).astype(F32)
    lo = (r1 - mid).astype(BF16).astype(F32)
    return hi, mid, lo


def _fox_aug(c, sign_c, ones_first):
    lane = lax.broadcasted_iota(jnp.int32, c.shape, 1)
    hi, mid, lo = _split3(c)
    k3 = lane - 3 * (lane >= 3).astype(jnp.int32)
    cpart = jnp.where(k3 == 0, hi, jnp.where(k3 == 1, mid, lo)) * sign_c
    c_lanes = (lane >= 3) if ones_first else (lane < 3)
    a = jnp.where(lane < 6, jnp.where(c_lanes, cpart, 1.0), 0.0)
    return a.astype(BF16)


def _fox_kernel(q_ref, k_ref, v_ref, c_ref, o_ref, ka_ref, va_ref, qa_ref, m_ref, acc_ref, *, tq, tk, S):
    i = pl.program_id(2)
    rows = min(256, S)

    @pl.when(i == 0)
    def _():
        def body(r, carry):
            sl = pl.ds(pl.multiple_of(r * rows, rows), rows)
            ka_ref[sl, :LANES] = k_ref[0, sl, :]
            ka_ref[sl, LANES:] = _fox_aug(c_ref[0, 0, sl, :] * LOG2E, -1.0, True)
            va_ref[sl, :LANES] = v_ref[0, sl, :]
            va_ref[sl, LANES:] = jnp.ones((rows, LANES), BF16)
            return carry
        lax.fori_loop(0, S // rows, body, 0)

    q_rows = pl.ds(pl.multiple_of(i * tq, tq), tq)
    qa_ref[:, :LANES] = (q_ref[0].astype(F32) * (HEAD_DIM ** -0.5 * LOG2E)).astype(BF16)
    qa_ref[:, LANES:] = _fox_aug(c_ref[0, 0, q_rows, :] * LOG2E, 1.0, False)
    m_ref[...] = jnp.full(m_ref.shape, -jnp.inf, F32)
    acc_ref[...] = jnp.zeros(acc_ref.shape, F32)

    def block(j, r0, masked):
        nr = tq - r0
        ks = pl.ds(pl.multiple_of(j * tk, tk), tk)
        s = lax.dot_general(qa_ref[r0:, :], ka_ref[ks, :], (((1,), (1,)), ((), ())),
                            preferred_element_type=F32)
        if masked:
            r = lax.broadcasted_iota(jnp.int32, (nr, tk), 0) + (i * tq + r0)
            c = lax.broadcasted_iota(jnp.int32, (nr, tk), 1) + j * tk
            s = jnp.where(c <= r, s, -jnp.inf)
        m_prev = m_ref[r0:, :]
        m_new = jnp.maximum(m_prev, jnp.max(s, axis=-1, keepdims=True))
        m_ref[r0:, :] = m_new
        p = jnp.exp2(s - jnp.concatenate([m_new] * (tk // LANES), axis=1)).astype(BF16)
        alpha = jnp.exp2(m_prev - m_new)
        acc_ref[r0:, :] = (jnp.concatenate([alpha, alpha], axis=1) * acc_ref[r0:, :]
                           + jnp.dot(p, va_ref[ks, :], preferred_element_type=F32))

    npb = tq // tk

    def full_body(jb, carry):
        for d in range(npb):
            block(jb * npb + d, 0, False)
        return carry

    lax.fori_loop(0, i, full_body, 0)
    for d in range(npb):
        block(i * npb + d, d * tk, True)
    acc = acc_ref[...]
    o_ref[...] = (acc[:, :HEAD_DIM] / acc[:, HEAD_DIM:]).astype(BF16)


def fox_attention(big, c, *, B, S, tq, tk):
    M = B * S
    nq = S // tq
    nh = FOX_HEADS
    q0 = 4 * GDN_HEADS
    k0 = q0 + nh
    v0 = k0 + nh
    return pl.pallas_call(
        functools.partial(_fox_kernel, tq=tq, tk=tk, S=S),
        grid=(B, nh, nq),
        in_specs=[pl.BlockSpec((1, tq, LANES), lambda b, h, i: (q0 + h, b * nq + i, 0)),
                  pl.BlockSpec((1, S, LANES), lambda b, h, i: (k0 + h, b, 0)),
                  pl.BlockSpec((1, S, LANES), lambda b, h, i: (v0 + h, b, 0)),
                  pl.BlockSpec((1, 1, S, LANES), lambda b, h, i: (b, h, 0, 0))],
        out_specs=pl.BlockSpec((tq, LANES), lambda b, h, i: (b * nq + i, h)),
        out_shape=jax.ShapeDtypeStruct((M, nh * HEAD_DIM), BF16),
        scratch_shapes=[pltpu.VMEM((S, 2 * LANES), BF16),
                        pltpu.VMEM((S, 2 * LANES), BF16),
                        pltpu.VMEM((tq, 2 * LANES), BF16),
                        pltpu.VMEM((tq, LANES), F32),
                        pltpu.VMEM((tq, 2 * HEAD_DIM), F32)],
        compiler_params=_cparams(3),
        name="fox_attention",
    )(big, big, big, c)


def _out_proj_kernel(og_ref, of_ref, w1_ref, w2_ref, h_ref, o_ref):
    acc = jnp.dot(og_ref[...], w1_ref[...], preferred_element_type=F32)
    acc = acc + jnp.dot(of_ref[...], w2_ref[...], preferred_element_type=F32)
    o_ref[...] = h_ref[...] + acc


def out_proj(og, of, w_out, h, *, tm, tn):
    M, D = h.shape
    Kg = og.shape[1]
    return pl.pallas_call(
        _out_proj_kernel,
        grid=(M // tm, D // tn),
        in_specs=[pl.BlockSpec((tm, Kg), lambda i, j: (i, 0)),
                  pl.BlockSpec((tm, Kg), lambda i, j: (i, 0)),
                  pl.BlockSpec((Kg, tn), lambda i, j: (0, j)),
                  pl.BlockSpec((Kg, tn), lambda i, j: (1, j)),
                  pl.BlockSpec((tm, tn), lambda i, j: (i, j))],
        out_specs=pl.BlockSpec((tm, tn), lambda i, j: (i, j)),
        out_shape=jax.ShapeDtypeStruct((M, D), F32),
        compiler_params=_cparams(2),
        name="out_proj",
    )(og, of, w_out, w_out, h)


def _mlp_kernel(h_ref, nw_ref, wu_ref, wd_ref, fw_ref, o_ref, hn_ref, *, final_norm):
    f = pl.program_id(1)

    @pl.when(f == 0)
    def _():
        _norm_to(h_ref, nw_ref, hn_ref, 32)
        o_ref[...] = h_ref[...]

    a = jnp.dot(hn_ref[...], wu_ref[...], preferred_element_type=F32)
    a = jnp.maximum(a, 0.0)
    a = (a * a).astype(BF16)
    o_ref[...] += jnp.dot(a, wd_ref[...], preferred_element_type=F32)

    if final_norm:
        @pl.when(f == pl.num_programs(1) - 1)
        def _():
            _norm_to(o_ref, fw_ref, o_ref, 32)


def mlp(h, nw, w_up, w_down, fw, *, tm, tf, final_norm):
    M, D = h.shape
    F = w_up.shape[1]
    return pl.pallas_call(
        functools.partial(_mlp_kernel, final_norm=final_norm),
        grid=(M // tm, F // tf),
        in_specs=[pl.BlockSpec((tm, D), lambda i, f: (i, 0)),
                  pl.BlockSpec((1, D), lambda i, f: (0, 0)),
                  pl.BlockSpec((D, tf), lambda i, f: (0, f)),
                  pl.BlockSpec((tf, D), lambda i, f: (f, 0)),
                  pl.BlockSpec((1, D), lambda i, f: (0, 0))],
        out_specs=pl.BlockSpec((tm, D), lambda i, f: (i, 0)),
        out_shape=jax.ShapeDtypeStruct((M, D), F32),
        scratch_shapes=[pltpu.VMEM((tm, D), BF16)],
        compiler_params=_cparams(2),
        name="mlp",
    )(h, nw, w_up, w_down, fw)


def _rmsnorm_kernel(h_ref, nw_ref, o_ref):
    _norm_to(h_ref, nw_ref, o_ref, 32)


def rmsnorm(h, nw, *, tm):
    M, D = h.shape
    return pl.pallas_call(
        _rmsnorm_kernel,
        grid=(M // tm,),
        in_specs=[pl.BlockSpec((tm, D), lambda i: (i, 0)),
                  pl.BlockSpec((1, D), lambda i: (0, 0))],
        out_specs=pl.BlockSpec((tm, D), lambda i: (i, 0)),
        out_shape=jax.ShapeDtypeStruct((M, D), F32),
        compiler_params=_cparams(1),
        name="rmsnorm",
    )(h, nw)


def _s5_assemble(bd_ref, pe_ref, po_ref, q_ref, wt_ref, wi_ref, wo_ref):
    L = S5_L
    P = S5_STATE
    half = LANES // 2
    zero_tile = jnp.zeros((LANES, LANES), BF16)
    for s in range(L):
        for t in range(L):
            wt_ref[s * LANES:(s + 1) * LANES, t * LANES:(t + 1) * LANES] = (
                bd_ref[0, t - s] if t >= s else zero_tile)
    rgrp = lax.broadcasted_iota(jnp.int32, (LANES, LANES), 0) >> 4
    lane = lax.broadcasted_iota(jnp.int32, (LANES, LANES), 1)
    for s in range(L):
        pe = pe_ref[0, s]
        po = po_ref[0, s]
        for r in range(2):
            lo_src, hi_src = (pe, po) if r == 0 else (po, pe)
            for qd in range(S5_GPT // 2):
                tile = jnp.where((rgrp == 2 * qd) & (lane < half), lo_src,
                                 jnp.where((rgrp == 2 * qd + 1) & (lane >= half), hi_src, zero_tile))
                c0 = (r * (S5_GPT // 2) + qd) * LANES
                wi_ref[s * LANES:(s + 1) * LANES, c0:c0 + LANES] = tile
    lgrp = lax.broadcasted_iota(jnp.int32, (P, LANES), 1) >> 4
    zero_rows = jnp.zeros((P, LANES), BF16)
    for r in range(2):
        for t in range(L):
            src = q_ref[0, r, t]
            for h in range(S5_GPT):
                r0 = (r * S5_GPT + h) * P
                wo_ref[r0:r0 + P, t * LANES:(t + 1) * LANES] = jnp.where(lgrp == h, src, zero_rows)


def _s5_kernel(u_ref, bd_ref, pe_ref, po_ref, q_ref, ap_ref, d_ref, y_ref,
               wt_ref, wi_ref, wo_ref, x_ref, yn_ref, carry_ref, *, ts):
    L = S5_L
    ncb = ts // L
    nst = S5_GPT * S5_STATE

    @pl.when((pl.program_id(1) == 0) & (pl.program_id(2) == 0))
    def _():
        _s5_assemble(bd_ref, pe_ref, po_ref, q_ref, wt_ref, wi_ref, wo_ref)

    @pl.when(pl.program_id(2) == 0)
    def _():
        carry_ref[...] = jnp.zeros_like(carry_ref)

    for s in range(L):
        x_ref[:, s * LANES:(s + 1) * LANES] = u_ref[pl.ds(s, ncb, stride=L), :].astype(BF16)
    xb = x_ref[...]
    y_intra = jnp.dot(xb, wt_ref[...], preferred_element_type=F32)
    v = jnp.dot(xb, wi_ref[...], preferred_element_type=F32)

    rows = lax.broadcasted_iota(jnp.int32, (ncb, nst), 0)
    vr, vi = v[:, :nst], v[:, nst:]
    er = jnp.where(rows == 0, carry_ref[0:1, :], pltpu.roll(vr, 1, axis=0))
    ei = jnp.where(rows == 0, carry_ref[1:2, :], pltpu.roll(vi, 1, axis=0))
    d = 1
    k = 1
    while d < ncb:
        ar = ap_ref[0, k:k + 1, :nst]
        ai = ap_ref[0, k:k + 1, nst:]
        sr = jnp.where(rows >= d, pltpu.roll(er, d, axis=0), 0.0)
        si = jnp.where(rows >= d, pltpu.roll(ei, d, axis=0), 0.0)
        er, ei = er + ar * sr - ai * si, ei + ar * si + ai * sr
        d *= 2
        k += 1
    a1r = ap_ref[0, 0:1, :nst]
    a1i = ap_ref[0, 0:1, nst:]
    lr, li = er[ncb - 1:ncb, :], ei[ncb - 1:ncb, :]
    carry_ref[0:1, :] = a1r * lr - a1i * li + vr[ncb - 1:ncb, :]
    carry_ref[1:2, :] = a1r * li + a1i * lr + vi[ncb - 1:ncb, :]

    hprev = jnp.concatenate([er, ei], axis=1).astype(BF16)
    y = y_intra + jnp.dot(hprev, wo_ref[...], preferred_element_type=F32)
    dsk = d_ref[0]
    for s in range(L):
        rs = pl.ds(s, ncb, stride=L)
        yn_ref[rs, :] = y[:, s * LANES:(s + 1) * LANES] + dsk * u_ref[rs, :]
    yn = yn_ref[...]
    c0 = math.sqrt(2.0 / math.pi)
    y_ref[...] = (0.5 * yn * (1.0 + jnp.tanh(c0 * (yn + 0.044715 * (yn * yn * yn))))).astype(BF16)


def s5_core(u, bd, pe, po, qo, apow, dskip, *, B, S, ts):
    M, D = u.shape
    nj = D // LANES
    ns = S // ts
    ncb = ts // S5_L
    nst = S5_GPT * S5_STATE
    lw = S5_L * LANES
    return pl.pallas_call(
        functools.partial(_s5_kernel, ts=ts),
        grid=(nj, B, ns),
        in_specs=[pl.BlockSpec((ts, LANES), lambda j, b, s: (b * ns + s, j)),
                  pl.BlockSpec((1, S5_L, LANES, LANES), lambda j, b, s: (j, 0, 0, 0)),
                  pl.BlockSpec((1, S5_L, LANES, LANES), lambda j, b, s: (j, 0, 0, 0)),
                  pl.BlockSpec((1, S5_L, LANES, LANES), lambda j, b, s: (j, 0, 0, 0)),
                  pl.BlockSpec((1, 2, S5_L, S5_STATE, LANES), lambda j, b, s: (j, 0, 0, 0, 0)),
                  pl.BlockSpec((1, apow.shape[1], 2 * nst), lambda j, b, s: (j, 0, 0)),
                  pl.BlockSpec((1, 1, LANES), lambda j, b, s: (j, 0, 0))],
        out_specs=pl.BlockSpec((ts, LANES), lambda j, b, s: (b * ns + s, j)),
        out_shape=jax.ShapeDtypeStruct((M, D), BF16),
        scratch_shapes=[pltpu.VMEM((lw, lw), BF16),
                        pltpu.VMEM((lw, 2 * nst), BF16),
                        pltpu.VMEM((2 * nst, lw), BF16),
                        pltpu.VMEM((ncb, lw), BF16),
                        pltpu.VMEM((ts, LANES), F32),
                        pltpu.VMEM((8, nst), F32)],
        compiler_params=_cparams(3),
        name="s5_core",
    )(u, bd, pe, po, qo, apow, dskip)


def s5_operators(lam_re, lam_im, b_re, b_im, c_re, c_im, d_skip, log_dt, ncb):
    G, P = lam_re.shape
    L = S5_L
    gc = S5_GROUP
    lam = lax.complex(jnp.minimum(lam_re.astype(F32), -1e-4), lam_im.astype(F32))
    dt = jnp.exp(log_dt.astype(F32))[:, None]
    lam_dt = lam * dt
    lam_bar = jnp.exp(lam_dt)
    b_bar = ((lam_bar - 1.0) / lam)[:, :, None] * lax.complex(b_re.astype(F32), b_im.astype(F32))
    c = lax.complex(c_re.astype(F32), c_im.astype(F32))
    kk = jnp.arange(L + 1, dtype=F32)
    pows = jnp.exp(lam_dt[None] * kk[:, None, None])

    nj = G // S5_GPT
    kern = jnp.einsum("gcp,kgp,gpd->kgcd", c, pows[:L], b_bar).real
    eye = jnp.eye(S5_GPT, dtype=F32)
    bd = jnp.einsum("kjgcd,gh->jkgdhc", kern.reshape(L, nj, S5_GPT, gc, gc), eye)
    bd = bd.reshape(nj, L, LANES, LANES).astype(BF16)

    sidx = jnp.arange(L)
    min_c = pows[L - 1 - sidx][:, :, :, None] * b_bar[None]
    min_c = min_c.transpose(1, 0, 3, 2).reshape(nj, S5_GPT, L, gc, P).transpose(0, 2, 1, 3, 4)
    pe = jnp.concatenate([min_c.real, min_c.imag], axis=-1).reshape(nj, L, LANES, 2 * P).astype(BF16)
    po = jnp.concatenate([min_c.imag, min_c.real], axis=-1).reshape(nj, L, LANES, 2 * P).astype(BF16)

    mout_c = c[None] * pows[1:L + 1][:, :, None, :]
    mout_c = mout_c.reshape(L, nj, S5_GPT, gc, P).transpose(1, 0, 4, 2, 3)
    qo = jnp.stack([mout_c.real, -mout_c.imag], axis=1).reshape(nj, 2, L, P, LANES).astype(BF16)

    nsteps = max(1, int(math.log2(ncb)))
    mult = jnp.asarray([1.0] + [2.0 ** k for k in range(nsteps)], F32) * L
    ap = jnp.exp(lam_dt[None] * mult[:, None, None])
    ap = jnp.stack([ap.real, ap.imag], axis=1).reshape(1 + nsteps, 2, nj, S5_GPT * P)
    ap = ap.transpose(2, 0, 1, 3).reshape(nj, 1 + nsteps, 2 * S5_GPT * P)
    pad = (-ap.shape[1]) % 8
    ap = jnp.pad(ap, ((0, 0), (0, pad), (0, 0)))
    dsk = d_skip.astype(F32).reshape(nj, 1, LANES)
    return bd, pe, po, qo, ap, dsk


def _glu_kernel(y_ref, wv_ref, wg_ref, h_ref, o_ref):
    y = y_ref[...]
    val = jnp.dot(y, wv_ref[...], preferred_element_type=F32)
    gate = jnp.dot(y, wg_ref[...], preferred_element_type=F32)
    o_ref[...] = h_ref[...] + val * jax.nn.sigmoid(gate)


def glu_proj(y, w_glu, h, *, tm, tn):
    M, D = h.shape
    K = y.shape[1]
    ng = D // tn
    return pl.pallas_call(
        _glu_kernel,
        grid=(M // tm, ng),
        in_specs=[pl.BlockSpec((tm, K), lambda i, j: (i, 0)),
                  pl.BlockSpec((K, tn), lambda i, j: (0, j)),
                  pl.BlockSpec((K, tn), lambda i, j: (0, ng + j)),
                  pl.BlockSpec((tm, tn), lambda i, j: (i, j))],
        out_specs=pl.BlockSpec((tm, tn), lambda i, j: (i, j)),
        out_shape=jax.ShapeDtypeStruct((M, D), F32),
        compiler_params=_cparams(2),
        name="glu_proj",
    )(y, w_glu, w_glu, h)


def _tile(n, pref):
    t = min(n, pref)
    assert n % t == 0, (n, pref)
    return t


def _pack_in_proj(w_in):
    gw = GDN_HEADS * HEAD_DIM
    fw = FOX_HEADS * HEAD_DIM
    o = 4 * gw
    big = jnp.concatenate([w_in[:, :o], w_in[:, o + 2 * GDN_HEADS:o + 2 * GDN_HEADS + 3 * fw]], axis=1)
    small = jnp.concatenate([w_in[:, o:o + 2 * GDN_HEADS], w_in[:, o + 2 * GDN_HEADS + 3 * fw:]], axis=1)
    small = jnp.pad(small, ((0, 0), (0, LANES - small.shape[1])))
    return big.astype(BF16), small.astype(BF16)


def _gate_params(a_log, dt_bias, f_bias):
    prm = jnp.zeros((8, LANES), F32)
    prm = prm.at[0, SM_A:SM_A + GDN_HEADS].set(a_log.astype(F32))
    prm = prm.at[1, SM_A:SM_A + GDN_HEADS].set(dt_bias.astype(F32))
    prm = prm.at[2, SM_F:SM_F + FOX_HEADS].set(f_bias.astype(F32))
    return prm


def kernel(x, norm_mix, norm_mlp, norm_final, w_in, conv_qkv, gdn_a_log, gdn_dt_bias, gdn_norm, fox_f_bias, w_out, s5_lambda_re, s5_lambda_im, s5_b_re, s5_b_im, s5_c_re, s5_c_im, s5_d, s5_log_dt, w_glu, w_up, w_down):
    B, S, D = x.shape
    M = B * S
    depth = norm_mix.shape[0]
    h = x.reshape(M, D).astype(F32)

    tm = _tile(M, 512)
    tm_wide = _tile(M, 1024)
    tn = _tile(D, 512)
    tf = _tile(w_up.shape[2], 512)
    tp = _tile(S, 256)
    tg = _tile(S, 512)
    tq = _tile(S, 2048)
    tk = _tile(tq, 256)
    ts = _tile(S, 4096)
    ncb = ts // S5_L

    for layer in range(depth):
        i = layer // 2
        nw = norm_mix[layer].reshape(1, D).astype(F32)
        if layer % 2 == 0:
            wbig, wsmall = _pack_in_proj(w_in[i])
            big, small = norm_proj(h, nw, wbig, wsmall, tm=tm, tn=_tile(wbig.shape[1], 1024))
            cw = conv_qkv[i].astype(F32).reshape(CONV_WIDTH, 3 * GDN_HEADS, LANES)
            prm = _gate_params(gdn_a_log[i], gdn_dt_bias[i], fox_f_bias[i])
            q, qg, k, kw, kd, vb, db, dc, el, cf = gdn_prep(big, small, cw, prm, B=B, S=S, tp=tp)
            o_gdn = gdn_core(q, qg, k, kw, kd, vb, big, db, dc, el,
                             gdn_norm[i].reshape(1, HEAD_DIM).astype(F32), B=B, S=S, tg=tg, hb=GDN_HB)
            o_fox = fox_attention(big, cf, B=B, S=S, tq=tq, tk=tk)
            h = out_proj(o_gdn, o_fox, w_out[i].astype(BF16), h, tm=tm_wide, tn=tn)
        else:
            u = rmsnorm(h, nw, tm=tm)
            ops = s5_operators(s5_lambda_re[i], s5_lambda_im[i], s5_b_re[i], s5_b_im[i],
                               s5_c_re[i], s5_c_im[i], s5_d[i], s5_log_dt[i], ncb)
            y = s5_core(u, *ops, B=B, S=S, ts=ts)
            h = glu_proj(y, w_glu[i].astype(BF16), h, tm=tm_wide, tn=tn)
        last = layer == depth - 1
        h = mlp(h, norm_mlp[layer].reshape(1, D).astype(F32), w_up[layer].astype(BF16),
                w_down[layer].astype(BF16), norm_final.reshape(1, D).astype(F32),
                tm=tm, tf=tf, final_norm=last)
    return h.reshape(B, S, D).astype(x.dtype)
```
